```python
import math
import jax, jax.numpy as jnp
from jax import lax
import numpy as np

D_MODEL = 2048
BATCH = 2
SEQ = 16384
DEPTH = 2

GRID_W = 64
CTX_LEN = 256
N_EVEN = (DEPTH + 1) // 2
N_ODD = DEPTH // 2
EPS = 1e-6

SSD_WIDTH = D_MODEL
SSD_HEAD_DIM = 64
SSD_HEADS = SSD_WIDTH // SSD_HEAD_DIM
SSD_GROUPS = 8
SSD_STATE = 128
SSD_CONV = 4
SSD_CHUNK = 128
SSD_GN = SSD_GROUPS * SSD_STATE
SSD_CONV_CH = SSD_WIDTH + 2 * SSD_GN
SSD_IN = SSD_WIDTH + SSD_CONV_CH + 2 * SSD_HEADS

POOL_WIDTH = D_MODEL
POOL_WINDOWS = (2, 4, 8, 16)
N_POOL = len(POOL_WINDOWS)
POOL_GROUP = POOL_WIDTH // N_POOL

EVEN_IN = SSD_IN + POOL_WIDTH
EVEN_MIX = SSD_WIDTH + POOL_WIDTH

CG_WIDTH = 2 * D_MODEL
CG_HEADS = 8
CG_HEAD_DIM = CG_WIDTH // CG_HEADS
CG_CHUNK = 128

N_EXPERTS = 32
TOP_K = 4
D_EXPERT = D_MODEL
SWIGLU_ALPHA = 1.702
SWIGLU_LIMIT = 7.0
MOE_BLOCK = 128

kernel_name = "hybrid_ssd_pool_gmlp_moe_dit_block"


def rms_norm(x):
    xf = x.astype(jnp.float32)
    return (xf * lax.rsqrt(jnp.mean(xf * xf, axis=-1, keepdims=True) + EPS)).astype(x.dtype)


def layer_norm(x, g, b):
    xf = x.astype(jnp.float32)
    mu = jnp.mean(xf, axis=-1, keepdims=True)
    var = jnp.mean(jnp.square(xf - mu), axis=-1, keepdims=True)
    return ((xf - mu) * lax.rsqrt(var + EPS)).astype(x.dtype) * g + b


def dw_conv_centred(u, w, b):
    C = u.shape[-1]
    K = w.shape[0]
    out = lax.conv_general_dilated(
        u, w[:, None, :].astype(u.dtype), window_strides=(1,),
        padding=[(K // 2, K - 1 - K // 2)],
        dimension_numbers=('NWC', 'WIO', 'NWC'), feature_group_count=C)
    return out + b


def ssd_chunked(xs, dt, a, bm, cm, h0):
    Bsz, L, H, P = xs.shape
    G, N = bm.shape[2], bm.shape[3]
    hpg = H // G
    nc = L // SSD_CHUNK
    f32 = jnp.float32
    xdt = (xs.astype(f32) * dt[..., None]).reshape(Bsz, nc, SSD_CHUNK, G, hpg, P)
    cum = jnp.cumsum((dt * a).reshape(Bsz, nc, SSD_CHUNK, G, hpg), axis=2)
    bc = bm.astype(f32).reshape(Bsz, nc, SSD_CHUNK, G, N)
    cc = cm.astype(f32).reshape(Bsz, nc, SSD_CHUNK, G, N)
    mask = jnp.tril(jnp.ones((SSD_CHUNK, SSD_CHUNK), bool))[:, :, None, None]
    seg = cum[:, :, :, None] - cum[:, :, None, :]
    decay = jnp.exp(jnp.where(mask, seg, -jnp.inf))
    scores = jnp.einsum('bclgn,bcsgn->bclsg', cc, bc)[..., None] * decay
    y = jnp.einsum('bclsgh,bcsghp->bclghp', scores, xdt)
    to_end = jnp.exp(cum[:, :, -1:] - cum)
    states = jnp.einsum('bclgn,bclghp->bcghpn', bc, xdt * to_end[..., None])
    chunk_decay = jnp.exp(cum[:, :, -1])

    def carry_fn(h, inp):
        s, d = inp
        return h * d[..., None, None] + s, h

    h_last, h_in = lax.scan(carry_fn, h0.astype(f32),
                            (jnp.moveaxis(states, 1, 0), jnp.moveaxis(chunk_decay, 1, 0)))
    h_in = jnp.moveaxis(h_in, 0, 1)
    y = y + jnp.einsum('bclgn,bcghpn->bclghp', cc, h_in) * jnp.exp(cum)[..., None]
    return y.reshape(Bsz, L, H, P).astype(xs.dtype), h_last


def ssd_stream(proj, conv_w, conv_b, a_log, dt_bias, d_skip, norm_w, h0):
    Bsz, L, _ = proj.shape
    z, xbc, dt_raw = jnp.split(proj, [SSD_WIDTH, SSD_WIDTH + SSD_CONV_CH], axis=-1)
    xbc = jax.nn.silu(dw_conv_centred(xbc, conv_w, conv_b))
    xs, bm, cm = jnp.split(xbc, [SSD_WIDTH, SSD_WIDTH + SSD_GN], axis=-1)
    xs = xs.reshape(Bsz, L, SSD_HEADS, SSD_HEAD_DIM)
    bm = bm.reshape(Bsz, L, SSD_GROUPS, SSD_STATE)
    cm = cm.reshape(Bsz, L, SSD_GROUPS, SSD_STATE)
    dt = jax.nn.softplus(dt_raw.astype(jnp.float32).reshape(Bsz, L, 2, SSD_HEADS)
                         + dt_bias.astype(jnp.float32))
    a = -jnp.exp(a_log.astype(jnp.float32))
    y_f, h_f = ssd_chunked(xs, dt[:, :, 0], a[0], bm, cm, h0[0])
    y_b, h_b = ssd_chunked(xs[:, ::-1], dt[:, ::-1, 1], a[1], bm[:, ::-1], cm[:, ::-1], h0[1])
    y = y_f + y_b[:, ::-1] + xs * d_skip[:, None]
    y = y.reshape(Bsz, L, SSD_WIDTH) * jax.nn.silu(z)
    y = rms_norm(y.reshape(Bsz, L, SSD_GROUPS, SSD_WIDTH // SSD_GROUPS)).reshape(Bsz, L, SSD_WIDTH)
    return y * norm_w, (h_f, h_b)


def window_mean(v, w, axis):
    L = v.shape[axis]
    pad = [(0, 0)] * v.ndim
    pad[axis] = (1, 0)
    cs = jnp.pad(jnp.cumsum(v.astype(jnp.float32), axis=axis), pad)
    pos = jnp.arange(L)
    lo = jnp.clip(pos - w // 2, 0, L)
    hi = jnp.clip(pos + w - w // 2, 0, L)
    s = jnp.take(cs, hi, axis=axis) - jnp.take(cs, lo, axis=axis)
    shape = [1] * v.ndim
    shape[axis] = L
    return (s / (hi - lo).astype(jnp.float32).reshape(shape)).astype(v.dtype)


def pool_mixer(v, pool_w, pool_b, pool_scale, rows):
    Bsz, L, _ = v.shape
    vg = v.reshape(Bsz, L, N_POOL, POOL_GROUP)
    outs = []
    for g, w in enumerate(POOL_WINDOWS):
        vi = vg[:, :, g]
        if rows is None:
            m = window_mean(vi, w, 1)
        else:
            m = vi.reshape(Bsz, rows, GRID_W, POOL_GROUP)
            m = window_mean(window_mean(m, w, 1), w, 2).reshape(Bsz, L, POOL_GROUP)
        outs.append(m - vi)
    d = jnp.stack(outs, axis=2)
    y = jnp.einsum('blgc,gcd->blgd', d, pool_w) + pool_b
    return y.reshape(Bsz, L, POOL_WIDTH) * pool_scale


def even_mixer(h_lat, h_ctx, w_in, conv_w, conv_b, a_log, dt_bias, d_skip, norm_w,
               pool_w, pool_b, pool_scale, w_out, rows, want_ctx):
    ssd_p = (conv_w, conv_b, a_log, dt_bias, d_skip, norm_w)
    p_lat = h_lat @ w_in
    p_ctx = h_ctx @ (w_in if want_ctx else w_in[:, :SSD_IN])
    h0 = jnp.zeros((h_ctx.shape[0], SSD_GROUPS, SSD_HEADS // SSD_GROUPS, SSD_HEAD_DIM, SSD_STATE),
                   jnp.float32)
    y_ctx, ctx_states = ssd_stream(p_ctx[..., :SSD_IN], *ssd_p, (h0, h0))
    y_lat, _ = ssd_stream(p_lat[..., :SSD_IN], *ssd_p, ctx_states)
    m_lat = jnp.concatenate(
        [y_lat, pool_mixer(p_lat[..., SSD_IN:], pool_w, pool_b, pool_scale, rows)], axis=-1) @ w_out
    if not want_ctx:
        return m_lat, None
    m_ctx = jnp.concatenate(
        [y_ctx, pool_mixer(p_ctx[..., SSD_IN:], pool_w, pool_b, pool_scale, None)], axis=-1) @ w_out
    return m_lat, m_ctx


def chunk_gmlp(h, w_in, ln_g, ln_b, w_s, b_s, w_out):
    Bsz, L, _ = h.shape
    u, v = jnp.split(jax.nn.gelu(h @ w_in, approximate=False), 2, axis=-1)
    v = layer_norm(v, ln_g, ln_b)
    vc = v.reshape(Bsz, L // CG_CHUNK, CG_CHUNK, CG_HEADS, CG_HEAD_DIM)
    sv = jnp.einsum('hlm,bnmhc->bnlhc', w_s, vc) + jnp.transpose(b_s)[:, :, None]
    return (u * sv.reshape(Bsz, L, CG_WIDTH)) @ w_out


def moe_ffn(h, router_w, router_b, w_gu, b_gu, w_down, b_down):
    T, Dm = h.shape
    logits = (h @ router_w).astype(jnp.float32) + router_b.astype(jnp.float32)
    top_v, top_i = lax.top_k(logits, TOP_K)
    gates = jax.nn.softmax(top_v, axis=-1)
    A = T * TOP_K
    flat_e = top_i.reshape(A)
    flat_t = jnp.repeat(jnp.arange(T, dtype=jnp.int32), TOP_K)
    flat_g = gates.reshape(A)
    order = jnp.argsort(flat_e)
    se = flat_e[order]
    counts = jnp.zeros((N_EXPERTS,), jnp.int32).at[flat_e].add(1)
    padded = (counts + MOE_BLOCK - 1) // MOE_BLOCK * MOE_BLOCK
    pad_end = jnp.cumsum(padded)
    pad_start = pad_end - padded
    start = jnp.cumsum(counts) - counts
    dest = pad_start[se] + jnp.arange(A, dtype=jnp.int32) - start[se]
    n_blocks = (A + MOE_BLOCK - 1) // MOE_BLOCK + N_EXPERTS
    P = n_blocks * MOE_BLOCK
    row_tok = jnp.zeros((P,), jnp.int32).at[dest].set(flat_t[order])
    row_gate = jnp.zeros((P,), jnp.float32).at[dest].set(flat_g[order])
    block_e = jnp.minimum(jnp.searchsorted(pad_end, jnp.arange(n_blocks) * MOE_BLOCK, side='right'),
                          N_EXPERTS - 1)

    def expert_block(args):
        toks, gts, e = args
        xb = h[toks]
        gu = xb @ w_gu[e] + b_gu[e]
        g_, u_ = jnp.split(gu, 2, axis=-1)
        g_ = jnp.minimum(g_, SWIGLU_LIMIT)
        u_ = jnp.clip(u_, -SWIGLU_LIMIT, SWIGLU_LIMIT)
        act = g_ * jax.nn.sigmoid(SWIGLU_ALPHA * g_) * (u_ + 1)
        return (act @ w_down[e] + b_down[e]) * gts[:, None].astype(h.dtype)

    ys = lax.map(expert_block, (row_tok.reshape(n_blocks, MOE_BLOCK),
                                row_gate.reshape(n_blocks, MOE_BLOCK), block_e))
    return jnp.zeros_like(h).at[row_tok].add(ys.reshape(P, Dm))


def setup_inputs(seed: int = 0) -> dict:
    key = jax.random.key(seed)
    ks = jax.random.split(key, 32)
    f32 = jnp.float32
    D = D_MODEL

    def nrm(k, shape, scale):
        return jax.random.normal(k, shape, f32) * scale

    dt0 = jnp.exp(jax.random.uniform(ks[9], (N_EVEN, 2, SSD_HEADS), f32,
                                     math.log(1e-3), math.log(1e-1)))
    return {
        "x": nrm(ks[0], (BATCH, SEQ, D), 1.0),
        "c": nrm(ks[1], (BATCH, D), 1.0),
        "ctx": nrm(ks[2], (BATCH, CTX_LEN, D), 1.0),
        "c_ctx": nrm(ks[3], (D,), 1.0),
        "ada_w": nrm(ks[4], (DEPTH, D, 6 * D), 0.5 * D ** -0.5),
        "ada_b": nrm(ks[5], (DEPTH, 6 * D), 0.02),
        "even_in_w": nrm(ks[6], (N_EVEN, D, EVEN_IN), D ** -0.5),
        "ssd_conv_w": nrm(ks[7], (N_EVEN, SSD_CONV, SSD_CONV_CH), SSD_CONV ** -0.5),
        "ssd_conv_b": nrm(ks[8], (N_EVEN, SSD_CONV_CH), 0.02),
        "ssd_a_log": jnp.log(jax.random.uniform(ks[10], (N_EVEN, 2, SSD_HEADS), f32, 1.0, 16.0)),
        "ssd_dt_bias": dt0 + jnp.log(-jnp.expm1(-dt0)),
        "ssd_d": 1.0 + nrm(ks[11], (N_EVEN, SSD_HEADS), 0.02),
        "ssd_norm_w": 1.0 + nrm(ks[12], (N_EVEN, SSD_WIDTH), 0.02),
        "pool_w": nrm(ks[13], (N_EVEN, N_POOL, POOL_GROUP, POOL_GROUP), POOL_GROUP ** -0.5),
        "pool_b": nrm(ks[14], (N_EVEN, N_POOL, POOL_GROUP), 0.02),
        "pool_scale": 1.0 + nrm(ks[15], (N_EVEN, POOL_WIDTH), 0.02),
        "even_out_w": nrm(ks[16], (N_EVEN, EVEN_MIX, D), EVEN_MIX ** -0.5),
        "cg_in_w": nrm(ks[17], (N_ODD, D, 2 * CG_WIDTH), D ** -0.5),
        "cg_ln_g": 1.0 + nrm(ks[18], (N_ODD, CG_WIDTH), 0.02),
        "cg_ln_b": nrm(ks[19], (N_ODD, CG_WIDTH), 0.02),
        "cg_ws": nrm(ks[20], (N_ODD, CG_HEADS, CG_CHUNK, CG_CHUNK), CG_CHUNK ** -0.5),
        "cg_bs": 1.0 + nrm(ks[21], (N_ODD, CG_HEADS, CG_CHUNK), 0.02),
        "cg_out_w": nrm(ks[22], (N_ODD, CG_WIDTH, D), CG_WIDTH ** -0.5),
        "router_w": nrm(ks[23], (DEPTH, D, N_EXPERTS), D ** -0.5),
        "router_b": nrm(ks[24], (DEPTH, N_EXPERTS), 0.01),
        "moe_w_gu": nrm(ks[25], (DEPTH, N_EXPERTS, D, 2 * D_EXPERT), D ** -0.5),
        "moe_b_gu": nrm(ks[26], (DEPTH, N_EXPERTS, 2 * D_EXPERT), 0.02),
        "moe_w_down": nrm(ks[27], (DEPTH, N_EXPERTS, D_EXPERT, D), D_EXPERT ** -0.5),
        "moe_b_down": nrm(ks[28], (DEPTH, N_EXPERTS, D), 0.02),
        "final_norm_w": 1.0 + nrm(ks[29], (D,), 0.02),
    }


def reference(x, c, ctx, c_ctx, ada_w, ada_b, even_in_w, ssd_conv_w, ssd_conv_b, ssd_a_log,
              ssd_dt_bias, ssd_d, ssd_norm_w, pool_w, pool_b, pool_scale, even_out_w,
              cg_in_w, cg_ln_g, cg_ln_b, cg_ws, cg_bs, cg_out_w, router_w, router_b,
              moe_w_gu, moe_b_gu, moe_w_down, moe_b_down, final_norm_w):
    D = x.shape[-1]
    rows = x.shape[1] // GRID_W
    c_act = jax.nn.silu(c)[:, None, :]
    cc_act = jax.nn.silu(c_ctx)
    for i in range(DEPTH):
        even = i % 2 == 0
        update_ctx = any(j % 2 == 0 for j in range(i + 1, DEPTH))
        need_ctx = even or update_ctx
        j = i // 2
        sh1, sc1, g1, sh2, sc2, g2 = jnp.split(c_act @ ada_w[i] + ada_b[i], 6, axis=-1)
        h_l = rms_norm(x) * (1 + sc1) + sh1
        h_c = None
        mod_c = None
        if need_ctx:
            n_mod = 6 if update_ctx else 2
            mod_c = jnp.split(cc_act @ ada_w[i, :, :n_mod * D] + ada_b[i, :n_mod * D], n_mod)
            h_c = rms_norm(ctx) * (1 + mod_c[1]) + mod_c[0]
        if even:
            m_l, m_c = even_mixer(h_l, h_c, even_in_w[j], ssd_conv_w[j], ssd_conv_b[j], ssd_a_log[j],
                                  ssd_dt_bias[j], ssd_d[j], ssd_norm_w[j], pool_w[j], pool_b[j],
                                  pool_scale[j], even_out_w[j], rows, update_ctx)
        else:
            cg_p = (cg_in_w[j], cg_ln_g[j], cg_ln_b[j], cg_ws[j], cg_bs[j], cg_out_w[j])
            m_l = chunk_gmlp(h_l, *cg_p)
            m_c = chunk_gmlp(h_c, *cg_p) if update_ctx else None
        x = x + g1 * m_l
        f_l = rms_norm(x) * (1 + sc2) + sh2
        moe_p = (router_w[i], router_b[i], moe_w_gu[i], moe_b_gu[i], moe_w_down[i], moe_b_down[i])
        if update_ctx:
            ctx = ctx + mod_c[2] * m_c
            f_c = rms_norm(ctx) * (1 + mod_c[4]) + mod_c[3]
            n_c = f_c.shape[0] * f_c.shape[1]
            out = moe_ffn(jnp.concatenate([f_c.reshape(n_c, D), f_l.reshape(-1, D)], axis=0), *moe_p)
            ctx = ctx + mod_c[5] * out[:n_c].reshape(ctx.shape)
            x = x + g2 * out[n_c:].reshape(x.shape)
        else:
            x = x + g2 * moe_ffn(f_l.reshape(-1, D), *moe_p).reshape(x.shape)
    return rms_norm(x) * final_norm_w
```

```python
import functools
import math

import numpy as np
import jax
import jax.numpy as jnp
from jax import lax
from jax.experimental import pallas as pl
from jax.experimental.pallas import tpu as pltpu

F32 = jnp.float32
BF16 = jnp.bfloat16
HIGHEST = lax.Precision.HIGHEST

GRID_W = 64
EPS = 1e-6
DEPTH = 2

SSD_HEAD_DIM = 64
SSD_GROUPS = 8
SSD_STATE = 128
SSD_CONV = 4
SSD_CHUNK = 128

POOL_WINDOWS = (2, 4, 8, 16)
POOL_HALO = 512

CG_HEADS = 8
CG_CHUNK = 128

N_EXPERTS = 32
TOP_K = 4
SWIGLU_ALPHA = 1.702
SWIGLU_LIMIT = 7.0
MOE_TM = 512

V7X_VMEM_LIMIT_BYTES = 56 * 1024 * 1024
BF16_SUBLANES = 16


def _params(*sem):
    return pltpu.CompilerParams(dimension_semantics=sem, vmem_limit_bytes=V7X_VMEM_LIMIT_BYTES)


def _mm_kernel(*refs, n_pairs, has_bias, pre_act, act, precision):
    o_ref = refs[-1]
    acc = None
    for p in range(n_pairs):
        xv = refs[2 * p][...]
        wv = refs[2 * p + 1][...]
        if precision is not None:
            xv = xv.astype(F32)
        if pre_act == "silu":
            xv = xv * (1.0 / (1.0 + jnp.exp(-xv)))
        d = jnp.dot(xv, wv, preferred_element_type=F32, precision=precision)
        acc = d if acc is None else acc + d
    if has_bias:
        acc = acc + refs[2 * n_pairs][...]
    if act == "gelu":
        acc = 0.5 * acc * (1.0 + lax.erf(acc * (1.0 / math.sqrt(2.0))))
    o_ref[...] = acc.astype(o_ref.dtype)


def _matmul(pairs, *, tm, tn, out_dtype, name, bias=None, pre_act=None, act=None, precision=None):
    M = pairs[0][0].shape[0]
    N = pairs[0][1].shape[1]
    tm = min(tm, M)
    tn = min(tn, N)
    assert M % tm == 0 and N % tn == 0
    in_specs, args = [], []
    for xa, wa in pairs:
        K = xa.shape[1]
        in_specs += [pl.BlockSpec((tm, K), lambda i, j: (i, 0)),
                     pl.BlockSpec((K, tn), lambda i, j: (0, j))]
        args += [xa, wa]
    if bias is not None:
        in_specs.append(pl.BlockSpec((1, tn), lambda i, j: (0, j)))
        args.append(bias.reshape(1, N))
    kern = functools.partial(_mm_kernel, n_pairs=len(pairs), has_bias=bias is not None,
                             pre_act=pre_act, act=act, precision=precision)
    return pl.pallas_call(
        kern, grid=(M // tm, N // tn), in_specs=in_specs,
        out_specs=pl.BlockSpec((tm, tn), lambda i, j: (i, j)),
        out_shape=jax.ShapeDtypeStruct((M, N), out_dtype),
        compiler_params=_params("parallel", "parallel"), name=name)(*args)


def _rms_mod(xv, sc, sh):
    ms = jnp.mean(xv * xv, axis=-1, keepdims=True)
    return xv * lax.rsqrt(ms + EPS) * (1.0 + sc) + sh


def _norm_kernel(x_ref, sc_ref, sh_ref, h_ref):
    h_ref[...] = _rms_mod(x_ref[...], sc_ref[0], sh_ref[0]).astype(h_ref.dtype)


def _norm_mod(x2, sc, sh, L, *, tl=512):
    T, D = x2.shape
    tl = min(tl, L)
    nl = L // tl
    mod = pl.BlockSpec((1, 1, D), lambda i: (i // nl, 0, 0))
    return pl.pallas_call(
        _norm_kernel, grid=(T // tl,),
        in_specs=[pl.BlockSpec((tl, D), lambda i: (i, 0)), mod, mod],
        out_specs=pl.BlockSpec((tl, D), lambda i: (i, 0)),
        out_shape=jax.ShapeDtypeStruct((T, D), BF16),
        compiler_params=_params("parallel"), name="norm_mod")(x2, sc, sh)


def _res_router_kernel(x_ref, m_ref, g_ref, sc_ref, sh_ref, rw_ref, rb_ref,
                       xo_ref, f_ref, ti_ref, tg_ref):
    xv = x_ref[...] + g_ref[0] * m_ref[...]
    xo_ref[...] = xv
    f = _rms_mod(xv, sc_ref[0], sh_ref[0])
    f_ref[...] = f
    lg = lax.dot_general(rw_ref[...], f, (((1,), (1,)), ((), ())),
                         precision=HIGHEST, preferred_element_type=F32) + rb_ref[...]
    iota = lax.broadcasted_iota(jnp.int32, lg.shape, 0)
    vals, idxs = [], []
    for _ in range(TOP_K):
        mx = jnp.max(lg, axis=0, keepdims=True)
        ix = jnp.min(jnp.where(lg == mx, iota, N_EXPERTS), axis=0, keepdims=True)
        vals.append(mx)
        idxs.append(ix)
        lg = jnp.where(iota == ix, -jnp.inf, lg)
    tv = jnp.concatenate(vals, axis=0)
    ex = jnp.exp(tv - vals[0])
    ti_ref[0] = jnp.concatenate(idxs, axis=0)
    tg_ref[0] = ex / jnp.sum(ex, axis=0, keepdims=True)


def _res_norm_router(x2, m, g, sc, sh, rw_t, rb, L, *, tl=512):
    T, D = x2.shape
    tl = min(tl, L)
    nl = L // tl
    nb = T // tl
    mod = pl.BlockSpec((1, 1, D), lambda i: (i // nl, 0, 0))
    row = pl.BlockSpec((tl, D), lambda i: (i, 0))
    tk = pl.BlockSpec((1, TOP_K, tl), lambda i: (i, 0, 0))
    xo, f, ti, tg = pl.pallas_call(
        _res_router_kernel, grid=(nb,),
        in_specs=[row, row, mod, mod, mod,
                  pl.BlockSpec((N_EXPERTS, D), lambda i: (0, 0)),
                  pl.BlockSpec((N_EXPERTS, 1), lambda i: (0, 0))],
        out_specs=[row, row, tk, tk],
        out_shape=[jax.ShapeDtypeStruct((T, D), F32), jax.ShapeDtypeStruct((T, D), F32),
                   jax.ShapeDtypeStruct((nb, TOP_K, tl), jnp.int32),
                   jax.ShapeDtypeStruct((nb, TOP_K, tl), F32)],
        compiler_params=_params("parallel"), name="res_norm_router")(x2, m, g, sc, sh, rw_t, rb)
    top_i = jnp.transpose(ti, (0, 2, 1)).reshape(T, TOP_K)
    gates = jnp.transpose(tg, (0, 2, 1)).reshape(T, TOP_K)
    return xo, f, top_i, gates


def _conv_kernel(prev_ref, main_ref, next_ref, w_ref, b_ref, o_ref):
    i = pl.program_id(1)
    nl = pl.num_programs(1)
    tl = main_ref.shape[0]
    hr = prev_ref.shape[0]
    pv = jnp.where(i > 0, prev_ref[...].astype(F32), 0.0)
    nx = jnp.where(i < nl - 1, next_ref[...].astype(F32), 0.0)
    ext = jnp.concatenate([pv, main_ref[...].astype(F32), nx], axis=0)
    n = ext.shape[0]
    acc = None
    for k in range(SSD_CONV):
        off = k - SSD_CONV // 2
        sh = (-off) % n
        term = (ext if sh == 0 else pltpu.roll(ext, sh, 0))[hr:hr + tl] * w_ref[k:k + 1, :]
        acc = term if acc is None else acc + term
    acc = acc + b_ref[...]
    o_ref[...] = (acc * (1.0 / (1.0 + jnp.exp(-acc)))).astype(o_ref.dtype)


def _ssd_conv(p_main, conv_w, conv_b, Bsz, L, *, tl=512):
    T = p_main.shape[0]
    W = p_main.shape[1] // 4
    tl = min(tl, L)
    nl = L // tl
    hr = BF16_SUBLANES
    lh = L // hr
    th = tl // hr
    return pl.pallas_call(
        _conv_kernel, grid=(Bsz, nl, 2),
        in_specs=[
            pl.BlockSpec((hr, W), lambda b, i, c: (jnp.maximum(b * lh + i * th - 1, b * lh), 1 + c)),
            pl.BlockSpec((tl, W), lambda b, i, c: (b * nl + i, 1 + c)),
            pl.BlockSpec((hr, W), lambda b, i, c: (jnp.minimum(b * lh + (i + 1) * th, (b + 1) * lh - 1), 1 + c)),
            pl.BlockSpec((SSD_CONV, W), lambda b, i, c: (0, c)),
            pl.BlockSpec((1, W), lambda b, i, c: (0, c)),
        ],
        out_specs=pl.BlockSpec((tl, W), lambda b, i, c: (b * nl + i, c)),
        out_shape=jax.ShapeDtypeStruct((T, 2 * W), BF16),
        compiler_params=_params("parallel", "parallel", "parallel"), name="ssd_conv",
    )(p_main, p_main, p_main, conv_w, conv_b.reshape(1, -1))


def _scan_kernel(xbc_ref, dt_ref, tri_ref, alog_ref, dtb_ref, h0_ref, y_ref, h_ref, *, width):
    c = pl.program_id(2)

    @pl.when(c == 0)
    def _():
        h_ref[...] = h0_ref[...]

    n_heads = width // SSD_HEAD_DIM
    hpg = n_heads // SSD_GROUPS
    gn = SSD_GROUPS * SSD_STATE
    raw = dt_ref[0] + dtb_ref[0]
    dt = jnp.maximum(raw, 0.0) + jnp.log1p(jnp.exp(-jnp.abs(raw)))
    da = dt * (-jnp.exp(alog_ref[0]))
    tri = tri_ref[0]
    cum = jnp.dot(tri, da, precision=HIGHEST, preferred_element_type=F32)
    cum_t = lax.dot_general(da, tri, (((0,), (1,)), ((), ())),
                            precision=HIGHEST, preferred_element_type=F32)
    tot = jnp.sum(da, axis=0, keepdims=True)
    e_cum = jnp.exp(cum)
    e_end = jnp.exp(tot - cum)
    e_tot = jnp.exp(tot)
    mask = tri > 0.0
    for g in range(SSD_GROUPS):
        bg = xbc_ref[:, width + g * SSD_STATE: width + (g + 1) * SSD_STATE]
        cg = xbc_ref[:, width + gn + g * SSD_STATE: width + gn + (g + 1) * SSD_STATE]
        gm = lax.dot_general(cg, bg, (((1,), (1,)), ((), ())), preferred_element_type=F32)
        for hh in range(hpg):
            h = g * hpg + hh
            seg = cum[:, h:h + 1] - cum_t[h:h + 1, :]
            dec = jnp.exp(jnp.where(mask, seg, -jnp.inf))
            sc = (gm * dec).astype(BF16)
            xh = xbc_ref[:, h * SSD_HEAD_DIM:(h + 1) * SSD_HEAD_DIM].astype(F32)
            xdt = xh * dt[:, h:h + 1]
            yh = jnp.dot(sc, xdt.astype(BF16), preferred_element_type=F32)
            hin = h_ref[0, 0, h]
            yoff = lax.dot_general(cg, hin.astype(BF16), (((1,), (1,)), ((), ())),
                                   preferred_element_type=F32)
            yh = yh + yoff * e_cum[:, h:h + 1]
            y_ref[0, :, h * SSD_HEAD_DIM:(h + 1) * SSD_HEAD_DIM] = yh.astype(y_ref.dtype)
            st = lax.dot_general((xdt * e_end[:, h:h + 1]).astype(BF16), bg,
                                 (((0,), (0,)), ((), ())), preferred_element_type=F32)
            h_ref[0, 0, h] = hin * e_tot[:, h:h + 1] + st


def _ssd_scan(xbc, dt2, tri, a_log, dt_bias, h0, Bsz, L, width):
    T = xbc.shape[0]
    H = width // SSD_HEAD_DIM
    nc = L // SSD_CHUNK

    def rows(d, b, c):
        return b * nc + c + d * (nc - 1 - 2 * c)

    st = pl.BlockSpec((1, 1, H, SSD_HEAD_DIM, SSD_STATE), lambda d, b, c: (d, b, 0, 0, 0))
    vec = pl.BlockSpec((1, 1, H), lambda d, b, c: (d, 0, 0))
    return pl.pallas_call(
        functools.partial(_scan_kernel, width=width), grid=(2, Bsz, nc),
        in_specs=[
            pl.BlockSpec((SSD_CHUNK, 2 * width), lambda d, b, c: (rows(d, b, c), 0)),
            pl.BlockSpec((1, SSD_CHUNK, H), lambda d, b, c: (d, rows(d, b, c), 0)),
            pl.BlockSpec((1, SSD_CHUNK, SSD_CHUNK), lambda d, b, c: (d, 0, 0)),
            vec, vec, st,
        ],
        out_specs=[pl.BlockSpec((1, SSD_CHUNK, width), lambda d, b, c: (d, rows(d, b, c), 0)), st],
        out_shape=[jax.ShapeDtypeStruct((2, T, width), BF16),
                   jax.ShapeDtypeStruct(h0.shape, F32)],
        compiler_params=_params("parallel", "parallel", "arbitrary"), name="ssd_scan",
    )(xbc, dt2, tri, a_log.reshape(2, 1, H), dt_bias.reshape(2, 1, H), h0)


def _ssd_out_kernel(y_ref, x_ref, z_ref, d_ref, nw_ref, o_ref):
    xs = x_ref[...].astype(F32)
    z = z_ref[...].astype(F32)
    y = y_ref[0].astype(F32) + y_ref[1].astype(F32) + xs * d_ref[...]
    y = y * (z * (1.0 / (1.0 + jnp.exp(-z))))
    gw = y.shape[1] // SSD_GROUPS
    for g in range(SSD_GROUPS):
        seg = y[:, g * gw:(g + 1) * gw]
        ms = jnp.mean(seg * seg, axis=-1, keepdims=True)
        o_ref[:, g * gw:(g + 1) * gw] = (seg * lax.rsqrt(ms + EPS) * nw_ref[:, g * gw:(g + 1) * gw]
                                         ).astype(o_ref.dtype)


def _ssd_out(y, xbc, p_main, d_skip, norm_w, *, tl=512):
    T = xbc.shape[0]
    W = xbc.shape[1] // 2
    tl = min(tl, T)
    row = lambda i: (i, 0)
    one = pl.BlockSpec((1, W), lambda i: (0, 0))
    return pl.pallas_call(
        _ssd_out_kernel, grid=(T // tl,),
        in_specs=[pl.BlockSpec((2, tl, W), lambda i: (0, i, 0)),
                  pl.BlockSpec((tl, W), row), pl.BlockSpec((tl, W), row), one, one],
        out_specs=pl.BlockSpec((tl, W), row),
        out_shape=jax.ShapeDtypeStruct((T, W), BF16),
        compiler_params=_params("parallel"), name="ssd_out",
    )(y, xbc, p_main, jnp.repeat(d_skip, SSD_HEAD_DIM).reshape(1, W), norm_w.reshape(1, W))


def _pool_kernel(prev_ref, v_ref, next_ref, band_ref, pw_ref, pb_ref, ps_ref, o_ref, *, rows):
    i = pl.program_id(1)
    nl = pl.num_programs(1)
    tl = v_ref.shape[0]
    gw = v_ref.shape[1] // len(POOL_WINDOWS)
    tile = band_ref.shape[1]
    pos = lax.broadcasted_iota(jnp.int32, (tl, 128), 0)
    r = i * (tl // GRID_W) + lax.shift_right_logical(pos, GRID_W.bit_length() - 1)
    cidx = lax.bitwise_and(pos, GRID_W - 1)
    for g, w in enumerate(POOL_WINDOWS):
        lo, hi = -(w // 2), w - 1 - w // 2
        cols = slice(g * gw, (g + 1) * gw)
        pv = jnp.where(i > 0, prev_ref[:, cols], jnp.zeros((), BF16))
        nx = jnp.where(i < nl - 1, next_ref[:, cols], jnp.zeros((), BF16))
        vg = v_ref[:, cols]
        ext = jnp.concatenate([pv, vg, nx], axis=0)
        band = band_ref[g]
        cs = jnp.concatenate(
            [jnp.dot(band, ext[t * tile:(t + 1) * tile], preferred_element_type=F32)
             for t in range(ext.shape[0] // tile)], axis=0)
        rs = None
        for o in range(lo, hi + 1):
            s0 = POOL_HALO + GRID_W * o
            term = cs[s0:s0 + tl]
            rs = term if rs is None else rs + term
        n_r = jnp.minimum(r + hi, rows - 1) - jnp.maximum(r + lo, 0) + 1
        n_c = jnp.minimum(cidx + hi, GRID_W - 1) - jnp.maximum(cidx + lo, 0) + 1
        inv = 1.0 / (n_r * n_c).astype(F32)
        inv = jnp.concatenate([inv] * (gw // 128), axis=1)
        d = rs * inv - vg.astype(F32)
        yv = jnp.dot(d.astype(BF16), pw_ref[g], preferred_element_type=F32) + pb_ref[:, cols]
        o_ref[:, cols] = (yv * ps_ref[:, cols]).astype(o_ref.dtype)


def _pool_bands():
    t = 2 * GRID_W
    i = np.arange(t)[:, None]
    j = np.arange(t)[None, :]
    out = []
    for w in POOL_WINDOWS:
        lo, hi = -(w // 2), w - 1 - w // 2
        dc = j % GRID_W - i % GRID_W
        out.append(((i // GRID_W == j // GRID_W) & (dc >= lo) & (dc <= hi)).astype(np.float32))
    return jnp.asarray(np.stack(out), dtype=BF16)


def _pool_mixer(p_main, pool_w, pool_b, pool_scale, Bsz, L, *, tl=1024):
    T = p_main.shape[0]
    W = p_main.shape[1] // 4
    tl = min(tl, L)
    assert L % tl == 0 and tl % POOL_HALO == 0
    nl = L // tl
    lh = L // POOL_HALO
    th = tl // POOL_HALO
    cb = 3
    full = lambda shape: pl.BlockSpec(shape, lambda b, i: (0,) * len(shape))
    return pl.pallas_call(
        functools.partial(_pool_kernel, rows=L // GRID_W), grid=(Bsz, nl),
        in_specs=[
            pl.BlockSpec((POOL_HALO, W), lambda b, i: (jnp.maximum(b * lh + i * th - 1, b * lh), cb)),
            pl.BlockSpec((tl, W), lambda b, i: (b * nl + i, cb)),
            pl.BlockSpec((POOL_HALO, W), lambda b, i: (jnp.minimum(b * lh + (i + 1) * th, (b + 1) * lh - 1), cb)),
            full((len(POOL_WINDOWS), 2 * GRID_W, 2 * GRID_W)),
            full(pool_w.shape), full((1, W)), full((1, W)),
        ],
        out_specs=pl.BlockSpec((tl, W), lambda b, i: (b * nl + i, 0)),
        out_shape=jax.ShapeDtypeStruct((T, W), BF16),
        compiler_params=_params("parallel", "parallel"), name="pool_mixer",
    )(p_main, p_main, p_main, _pool_bands(), pool_w.astype(BF16),
      pool_b.reshape(1, W), pool_scale.reshape(1, W))


def _gmlp_kernel(u_ref, v_ref, g_ref, b_ref, ws_ref, bs_ref, o_ref):
    v = v_ref[...].astype(F32)
    mu = jnp.mean(v, axis=-1, keepdims=True)
    vc = v - mu
    var = jnp.mean(vc * vc, axis=-1, keepdims=True)
    vn = (vc * lax.rsqrt(var + EPS) * g_ref[...] + b_ref[...]).astype(BF16)
    tl, W = v.shape
    hd = W // CG_HEADS
    for n in range(tl // CG_CHUNK):
        rs = slice(n * CG_CHUNK, (n + 1) * CG_CHUNK)
        for h in range(CG_HEADS):
            cs = slice(h * hd, (h + 1) * hd)
            sv = jnp.dot(ws_ref[h], vn[rs, cs], preferred_element_type=F32) + bs_ref[:, h:h + 1]
            o_ref[rs, cs] = (u_ref[rs, cs].astype(F32) * sv).astype(o_ref.dtype)


def _gmlp_gate(uv, ln_g, ln_b, ws, bs, *, tl=256):
    T = uv.shape[0]
    W = uv.shape[1] // 2
    tl = min(tl, T)
    one = pl.BlockSpec((1, W), lambda i: (0, 0))
    return pl.pallas_call(
        _gmlp_kernel, grid=(T // tl,),
        in_specs=[pl.BlockSpec((tl, W), lambda i: (i, 0)), pl.BlockSpec((tl, W), lambda i: (i, 1)),
                  one, one,
                  pl.BlockSpec(ws.shape, lambda i: (0, 0, 0)),
                  pl.BlockSpec((CG_CHUNK, CG_HEADS), lambda i: (0, 0))],
        out_specs=pl.BlockSpec((tl, W), lambda i: (i, 0)),
        out_shape=jax.ShapeDtypeStruct((T, W), BF16),
        compiler_params=_params("parallel"), name="gmlp_gate",
    )(uv, uv, ln_g.reshape(1, W), ln_b.reshape(1, W), ws.astype(BF16), jnp.transpose(bs))


def _dispatch_kernel(dest_ref, f_hbm, xs_in, xs_out, sem):
    del xs_in
    i = pl.program_id(0)
    tb = dest_ref.shape[2] // TOP_K

    def row_copy(t, k):
        return pltpu.make_async_copy(f_hbm.at[pl.ds(i * tb + t, 1), :],
                                     xs_out.at[pl.ds(dest_ref[0, 0, t * TOP_K + k], 1), :], sem)

    def issue(t, carry):
        for k in range(TOP_K):
            row_copy(t, k).start()
        return carry

    def drain(t, carry):
        for k in range(TOP_K):
            row_copy(t, k).wait()
        return carry

    lax.fori_loop(0, tb, issue, 0)
    lax.fori_loop(0, tb, drain, 0)


def _moe_dispatch(f, dest, P, *, tb=1024):
    T, D = f.shape
    tb = min(tb, T)
    xs0 = jnp.zeros((P, D), F32)
    return pl.pallas_call(
        _dispatch_kernel, grid=(T // tb,),
        in_specs=[pl.BlockSpec((1, 1, tb * TOP_K), lambda i: (i, 0, 0), memory_space=pltpu.SMEM),
                  pl.BlockSpec(memory_space=pl.ANY), pl.BlockSpec(memory_space=pl.ANY)],
        out_specs=pl.BlockSpec(memory_space=pl.ANY),
        out_shape=jax.ShapeDtypeStruct((P, D), F32),
        scratch_shapes=[pltpu.SemaphoreType.DMA(())],
        input_output_aliases={2: 0},
        compiler_params=_params("arbitrary"), name="moe_dispatch",
    )(dest.reshape(T // tb, 1, tb * TOP_K), f, xs0)


def _gu_kernel(be_ref, nu_ref, x_ref, wg_ref, wu_ref, bg_ref, bu_ref, o_ref):
    @pl.when(pl.program_id(1) < nu_ref[0])
    def _():
        xv = x_ref[...].astype(BF16)
        gv = jnp.dot(xv, wg_ref[0], preferred_element_type=F32) + bg_ref[0]
        uv = jnp.dot(xv, wu_ref[0], preferred_element_type=F32) + bu_ref[0]
        gv = jnp.minimum(gv, SWIGLU_LIMIT)
        uv = jnp.clip(uv, -SWIGLU_LIMIT, SWIGLU_LIMIT)
        act = gv * (1.0 / (1.0 + jnp.exp(-SWIGLU_ALPHA * gv))) * (uv + 1.0)
        o_ref[...] = act.astype(o_ref.dtype)


def _moe_gate_up(xs, block_e, n_used, w_gu, b_gu, *, tn=1024):
    P, D = xs.shape
    F = w_gu.shape[2] // 2
    nj = F // tn
    nb = P // MOE_TM
    b3 = b_gu.reshape(N_EXPERTS, 1, 2 * F)
    grid_spec = pltpu.PrefetchScalarGridSpec(
        num_scalar_prefetch=2, grid=(nj, nb),
        in_specs=[pl.BlockSpec((MOE_TM, D), lambda j, i, be, nu: (i, 0)),
                  pl.BlockSpec((1, D, tn), lambda j, i, be, nu: (be[i], 0, j)),
                  pl.BlockSpec((1, D, tn), lambda j, i, be, nu: (be[i], 0, nj + j)),
                  pl.BlockSpec((1, 1, tn), lambda j, i, be, nu: (be[i], 0, j)),
                  pl.BlockSpec((1, 1, tn), lambda j, i, be, nu: (be[i], 0, nj + j))],
        out_specs=pl.BlockSpec((MOE_TM, tn), lambda j, i, be, nu: (i, j)))
    return pl.pallas_call(
        _gu_kernel, grid_spec=grid_spec,
        out_shape=jax.ShapeDtypeStruct((P, F), BF16),
        compiler_params=_params("parallel", "parallel"), name="moe_gate_up",
    )(block_e, n_used, xs, w_gu, w_gu, b3, b3)


def _down_kernel(be_ref, nu_ref, a_ref, w_ref, b_ref, o_ref):
    @pl.when(pl.program_id(1) < nu_ref[0])
    def _():
        o_ref[...] = jnp.dot(a_ref[...], w_ref[0], preferred_element_type=F32) + b_ref[0]


def _moe_down(act, block_e, n_used, w_down, b_down, *, tn=1024):
    P, F = act.shape
    D = w_down.shape[2]
    nb = P // MOE_TM
    grid_spec = pltpu.PrefetchScalarGridSpec(
        num_scalar_prefetch=2, grid=(D // tn, nb),
        in_specs=[pl.BlockSpec((MOE_TM, F), lambda j, i, be, nu: (i, 0)),
                  pl.BlockSpec((1, F, tn), lambda j, i, be, nu: (be[i], 0, j)),
                  pl.BlockSpec((1, 1, tn), lambda j, i, be, nu: (be[i], 0, j))],
        out_specs=pl.BlockSpec((MOE_TM, tn), lambda j, i, be, nu: (i, j)))
    return pl.pallas_call(
        _down_kernel, grid_spec=grid_spec,
        out_shape=jax.ShapeDtypeStruct((P, D), F32),
        compiler_params=_params("parallel", "parallel"), name="moe_down",
    )(block_e, n_used, act, w_down, b_down.reshape(N_EXPERTS, 1, D))


def _combine_kernel(dest_ref, gt_ref, x_ref, g_ref, sc_ref, sh_ref, ys_hbm, *rest, want_x):
    if want_x:
        xo_ref, h_ref, buf, sem = rest
    else:
        h_ref, buf, sem = rest
    tb = x_ref.shape[0]

    def row_copy(t, k):
        return pltpu.make_async_copy(ys_hbm.at[pl.ds(dest_ref[0, 0, t * TOP_K + k], 1), :],
                                     buf.at[k, pl.ds(t, 1), :], sem)

    def issue(t, carry):
        for k in range(TOP_K):
            row_copy(t, k).start()
        return carry

    def drain(t, carry):
        for k in range(TOP_K):
            row_copy(t, k).wait()
        return carry

    lax.fori_loop(0, tb, issue, 0)
    lax.fori_loop(0, tb, drain, 0)
    moe = None
    for k in range(TOP_K):
        term = buf[k] * gt_ref[:, k:k + 1]
        moe = term if moe is None else moe + term
    xv = x_ref[...] + g_ref[0] * moe
    if want_x:
        xo_ref[...] = xv
    h_ref[...] = _rms_mod(xv, sc_ref[0], sh_ref[0]).astype(h_ref.dtype)


def _moe_combine(ys, dest, gates, x2, g, sc, sh, L, *, want_x, out_dtype, tb=256):
    T, D = x2.shape
    tb = min(tb, L)
    nl = L // tb
    mod = pl.BlockSpec((1, 1, D), lambda i: (i // nl, 0, 0))
    row = pl.BlockSpec((tb, D), lambda i: (i, 0))
    out_specs = [row, row] if want_x else [row]
    out_shape = ([jax.ShapeDtypeStruct((T, D), F32)] if want_x else []) + [jax.ShapeDtypeStruct((T, D), out_dtype)]
    return pl.pallas_call(
        functools.partial(_combine_kernel, want_x=want_x), grid=(T // tb,),
        in_specs=[pl.BlockSpec((1, 1, tb * TOP_K), lambda i: (i, 0, 0), memory_space=pltpu.SMEM),
                  pl.BlockSpec((tb, TOP_K), lambda i: (i, 0)),
                  row, mod, mod, mod, pl.BlockSpec(memory_space=pl.ANY)],
        out_specs=out_specs, out_shape=out_shape,
        scratch_shapes=[pltpu.VMEM((TOP_K, tb, D), F32), pltpu.SemaphoreType.DMA(())],
        compiler_params=_params("arbitrary"), name="moe_combine",
    )(dest.reshape(T // tb, 1, tb * TOP_K), gates, x2, g, sc, sh, ys)


def _moe_routing(top_i):
    T = top_i.shape[0]
    A = T * TOP_K
    P = A + N_EXPERTS * MOE_TM
    nb = P // MOE_TM
    flat_e = top_i.reshape(A)
    onehot = (flat_e[:, None] == jnp.arange(N_EXPERTS, dtype=jnp.int32)[None, :]).astype(jnp.int32)
    cs = jnp.cumsum(onehot, axis=0)
    rank = jnp.take_along_axis(cs, flat_e[:, None], axis=1)[:, 0] - 1
    counts = cs[-1]
    padded = (counts + MOE_TM - 1) // MOE_TM * MOE_TM
    pad_end = jnp.cumsum(padded)
    pad_start = pad_end - padded
    dest = (pad_start[flat_e] + rank).astype(jnp.int32)
    block_e = jnp.minimum(jnp.searchsorted(pad_end, jnp.arange(nb, dtype=jnp.int32) * MOE_TM, side="right"),
                          N_EXPERTS - 1).astype(jnp.int32)
    n_used = (pad_end[-1:] // MOE_TM).astype(jnp.int32)
    return dest, block_e, n_used, P


def _moe(x2, m, g1, sc2, sh2, g2, nsc, nsh, router_w, router_b, w_gu, b_gu, w_down, b_down, L,
         *, want_x, out_dtype):
    xo, f, top_i, gates = _res_norm_router(x2, m, g1, sc2, sh2, jnp.transpose(router_w),
                                           router_b.reshape(N_EXPERTS, 1), L)
    dest, block_e, n_used, P = _moe_routing(top_i)
    xs = _moe_dispatch(f, dest, P)
    act = _moe_gate_up(xs, block_e, n_used, w_gu.astype(BF16), b_gu)
    ys = _moe_down(act, block_e, n_used, w_down.astype(BF16), b_down)
    return _moe_combine(ys, dest, gates, xo, g2, nsc, nsh, L, want_x=want_x, out_dtype=out_dtype)


def _ssd_tri():
    l = np.arange(SSD_CHUNK)[:, None]
    j = np.arange(SSD_CHUNK)[None, :]
    return jnp.asarray(np.stack([(j <= l), (j >= l)]).astype(np.float32))


def _ssd_stream(h_bf, w_main, w_dt, conv_w, conv_b, a_log, dt_bias, h0, Bsz, L, width):
    T = h_bf.shape[0]
    H = width // SSD_HEAD_DIM
    p_main = _matmul([(h_bf, w_main)], tm=1024, tn=1024, out_dtype=BF16, name="even_in_proj")
    dt_raw = _matmul([(h_bf, w_dt)], tm=512, tn=128, out_dtype=F32, precision=HIGHEST, name="dt_proj")
    dt2 = jnp.transpose(dt_raw[:, :2 * H].reshape(T, 2, H), (1, 0, 2))
    xbc = _ssd_conv(p_main, conv_w, conv_b, Bsz, L)
    y, h_last = _ssd_scan(xbc, dt2, _ssd_tri(), a_log, dt_bias, h0, Bsz, L, width)
    return p_main, xbc, y, h_last


def kernel(x, c, ctx, c_ctx, ada_w, ada_b, even_in_w, ssd_conv_w, ssd_conv_b, ssd_a_log, ssd_dt_bias,
           ssd_d, ssd_norm_w, pool_w, pool_b, pool_scale, even_out_w, cg_in_w, cg_ln_g, cg_ln_b, cg_ws,
           cg_bs, cg_out_w, router_w, router_b, moe_w_gu, moe_b_gu, moe_w_down, moe_b_down, final_norm_w):
    Bsz, S, D = x.shape
    Lc = ctx.shape[1]
    T = Bsz * S
    width = D
    H = width // SSD_HEAD_DIM
    gn = SSD_GROUPS * SSD_STATE
    x2 = x.reshape(T, D)
    ctx2 = ctx.reshape(Bsz * Lc, D)

    cin = jnp.zeros((8, D), F32).at[:Bsz].set(c).at[Bsz].set(c_ctx)
    mods = [_matmul([(cin, ada_w[i])], tm=8, tn=1024, out_dtype=F32, bias=ada_b[i], pre_act="silu",
                    precision=HIGHEST, name="ada_mod") for i in range(DEPTH)]

    def lat_mods(i):
        return [mods[i][:Bsz, k * D:(k + 1) * D].reshape(Bsz, 1, D) for k in range(6)]

    sh1, sc1, g1, sh2, sc2, g2 = lat_mods(0)
    sh_c = jnp.broadcast_to(mods[0][Bsz, :D].reshape(1, 1, D), (Bsz, 1, D))
    sc_c = jnp.broadcast_to(mods[0][Bsz, D:2 * D].reshape(1, 1, D), (Bsz, 1, D))
    w_in = even_in_w[0]
    o_dt = 2 * width + 2 * gn
    w_main = jnp.concatenate([w_in[:, :o_dt], w_in[:, o_dt + 2 * H:]], axis=1).astype(BF16)
    w_dt = jnp.pad(w_in[:, o_dt:o_dt + 2 * H], ((0, 0), (0, 128 - 2 * H)))
    ssd_p = (w_main, w_dt, ssd_conv_w[0], ssd_conv_b[0], ssd_a_log[0], ssd_dt_bias[0])

    h_c = _norm_mod(ctx2, sc_c, sh_c, Lc)
    h0 = jnp.zeros((2, Bsz, H, SSD_HEAD_DIM, SSD_STATE), F32)
    _, _, _, ctx_state = _ssd_stream(h_c, *ssd_p, h0, Bsz, Lc, width)

    h_l = _norm_mod(x2, sc1, sh1, S)
    p_main, xbc, y, _ = _ssd_stream(h_l, *ssd_p, ctx_state, Bsz, S, width)
    y_ssd = _ssd_out(y, xbc, p_main, ssd_d[0], ssd_norm_w[0])
    y_pool = _pool_mixer(p_main, pool_w[0], pool_b[0], pool_scale[0], Bsz, S)
    w_out = even_out_w[0].astype(BF16)
    m = _matmul([(y_ssd, w_out[:width]), (y_pool, w_out[width:])], tm=1024, tn=1024, out_dtype=F32,
                name="even_out_proj")

    nsh1, nsc1, ng1, nsh2, nsc2, ng2 = lat_mods(1)
    x2, h_l = _moe(x2, m, g1, sc2, sh2, g2, nsc1, nsh1, router_w[0], router_b[0], moe_w_gu[0], moe_b_gu[0],
                   moe_w_down[0], moe_b_down[0], S, want_x=True, out_dtype=BF16)

    uv = _matmul([(h_l, cg_in_w[0].astype(BF16))], tm=1024, tn=1024, out_dtype=BF16, act="gelu",
                 name="cg_in_proj")
    gated = _gmlp_gate(uv, cg_ln_g[0], cg_ln_b[0], cg_ws[0], cg_bs[0])
    m = _matmul([(gated, cg_out_w[0].astype(BF16))], tm=1024, tn=1024, out_dtype=F32, name="cg_out_proj")
    fin_sc = jnp.broadcast_to((final_norm_w - 1.0).reshape(1, 1, D), (Bsz, 1, D))
    fin_sh = jnp.zeros((Bsz, 1, D), F32)
    (out,) = _moe(x2, m, ng1, nsc2, nsh2, ng2, fin_sc, fin_sh, router_w[1], router_b[1], moe_w_gu[1],
                  moe_b_gu[1], moe_w_down[1], moe_b_down[1], S, want_x=False, out_dtype=F32)
    return out.reshape(Bsz, S, D)
```

```python
import functools
import math

import numpy as np
import jax
import jax.numpy as jnp
from jax import lax
from jax.experimental import pallas as pl
from jax.experimental.pallas import tpu as pltpu

F32 = jnp.float32
BF16 = jnp.bfloat16
HIGHEST = lax.Precision.HIGHEST

GRID_W = 64
EPS = 1e-6
DEPTH = 2

SSD_HEAD_DIM = 64
SSD_GROUPS = 8
SSD_STATE = 128
SSD_CONV = 4
SSD_CHUNK = 128

POOL_WINDOWS = (2, 4, 8, 16)
POOL_HALO = 512

CG_HEADS = 8
CG_CHUNK = 128

N_EXPERTS = 32
TOP_K = 4
SWIGLU_ALPHA = 1.702
SWIGLU_LIMIT = 7.0
MOE_TM = 512

V7X_VMEM_LIMIT_BYTES = 56 * 1024 * 1024
BF16_SUBLANES = 16


def _params(*sem):
    return pltpu.CompilerParams(dimension_semantics=sem, vmem_limit_bytes=V7X_VMEM_LIMIT_BYTES)


def _mm_kernel(*refs, n_pairs, has_bias, pre_act, act, precision):
    o_ref = refs[-1]
    acc = None
    for p in range(n_pairs):
        xv = refs[2 * p][...]
        wv = refs[2 * p + 1][...]
        if precision is not None:
            xv = xv.astype(F32)
        if pre_act == "silu":
            xv = xv * (1.0 / (1.0 + jnp.exp(-xv)))
        d = jnp.dot(xv, wv, preferred_element_type=F32, precision=precision)
        acc = d if acc is None else acc + d
    if has_bias:
        acc = acc + refs[2 * n_pairs][...]
    if act == "gelu":
        acc = 0.5 * acc * (1.0 + lax.erf(acc * (1.0 / math.sqrt(2.0))))
    o_ref[...] = acc.astype(o_ref.dtype)


def _matmul(pairs, *, tm, tn, out_dtype, name, bias=None, pre_act=None, act=None, precision=None):
    M = pairs[0][0].shape[0]
    N = pairs[0][1].shape[1]
    tm = min(tm, M)
    tn = min(tn, N)
    assert M % tm == 0 and N % tn == 0
    in_specs, args = [], []
    for xa, wa in pairs:
        K = xa.shape[1]
        in_specs += [pl.BlockSpec((tm, K), lambda i, j: (i, 0)),
                     pl.BlockSpec((K, tn), lambda i, j: (0, j))]
        args += [xa, wa]
    if bias is not None:
        in_specs.append(pl.BlockSpec((1, tn), lambda i, j: (0, j)))
        args.append(bias.reshape(1, N))
    kern = functools.partial(_mm_kernel, n_pairs=len(pairs), has_bias=bias is not None,
                             pre_act=pre_act, act=act, precision=precision)
    return pl.pallas_call(
        kern, grid=(M // tm, N // tn), in_specs=in_specs,
        out_specs=pl.BlockSpec((tm, tn), lambda i, j: (i, j)),
        out_shape=jax.ShapeDtypeStruct((M, N), out_dtype),
        compiler_params=_params("parallel", "parallel"), name=name)(*args)


def _rms_mod(xv, sc, sh):
    ms = jnp.mean(xv * xv, axis=-1, keepdims=True)
    return xv * lax.rsqrt(ms + EPS) * (1.0 + sc) + sh


def _norm_kernel(x_ref, sc_ref, sh_ref, h_ref):
    h_ref[...] = _rms_mod(x_ref[...], sc_ref[0], sh_ref[0]).astype(h_ref.dtype)


def _norm_mod(x2, sc, sh, L, *, tl=512):
    T, D = x2.shape
    tl = min(tl, L)
    nl = L // tl
    mod = pl.BlockSpec((1, 1, D), lambda i: (i // nl, 0, 0))
    return pl.pallas_call(
        _norm_kernel, grid=(T // tl,),
        in_specs=[pl.BlockSpec((tl, D), lambda i: (i, 0)), mod, mod],
        out_specs=pl.BlockSpec((tl, D), lambda i: (i, 0)),
        out_shape=jax.ShapeDtypeStruct((T, D), BF16),
        compiler_params=_params("parallel"), name="norm_mod")(x2, sc, sh)


def _res_router_kernel(x_ref, m_ref, g_ref, sc_ref, sh_ref, rw_ref, rb_ref,
                       xo_ref, f_ref, ti_ref, tg_ref):
    xv = x_ref[...] + g_ref[0] * m_ref[...]
    xo_ref[...] = xv
    f = _rms_mod(xv, sc_ref[0], sh_ref[0])
    f_ref[...] = f
    lg = lax.dot_general(rw_ref[...], f, (((1,), (1,)), ((), ())),
                         precision=HIGHEST, preferred_element_type=F32) + rb_ref[...]
    iota = lax.broadcasted_iota(jnp.int32, lg.shape, 0)
    vals, idxs = [], []
    for _ in range(TOP_K):
        mx = jnp.max(lg, axis=0, keepdims=True)
        ix = jnp.min(jnp.where(lg == mx, iota, N_EXPERTS), axis=0, keepdims=True)
        vals.append(mx)
        idxs.append(ix)
        lg = jnp.where(iota == ix, -jnp.inf, lg)
    tv = jnp.concatenate(vals, axis=0)
    ex = jnp.exp(tv - vals[0])
    ti_ref[0] = jnp.concatenate(idxs, axis=0)
    tg_ref[0] = ex / jnp.sum(ex, axis=0, keepdims=True)


def _res_norm_router(x2, m, g, sc, sh, rw_t, rb, L, *, tl=512):
    T, D = x2.shape
    tl = min(tl, L)
    nl = L // tl
    nb = T // tl
    mod = pl.BlockSpec((1, 1, D), lambda i: (i // nl, 0, 0))
    row = pl.BlockSpec((tl, D), lambda i: (i, 0))
    tk = pl.BlockSpec((1, TOP_K, tl), lambda i: (i, 0, 0))
    xo, f, ti, tg = pl.pallas_call(
        _res_router_kernel, grid=(nb,),
        in_specs=[row, row, mod, mod, mod,
                  pl.BlockSpec((N_EXPERTS, D), lambda i: (0, 0)),
                  pl.BlockSpec((N_EXPERTS, 1), lambda i: (0, 0))],
        out_specs=[row, row, tk, tk],
        out_shape=[jax.ShapeDtypeStruct((T, D), F32), jax.ShapeDtypeStruct((T, D), F32),
                   jax.ShapeDtypeStruct((nb, TOP_K, tl), jnp.int32),
                   jax.ShapeDtypeStruct((nb, TOP_K, tl), F32)],
        compiler_params=_params("parallel"), name="res_norm_router")(x2, m, g, sc, sh, rw_t, rb)
    top_i = jnp.transpose(ti, (0, 2, 1)).reshape(T, TOP_K)
    gates = jnp.transpose(tg, (0, 2, 1)).reshape(T, TOP_K)
    return xo, f, top_i, gates


def _conv_kernel(prev_ref, main_ref, next_ref, w_ref, b_ref, o_ref):
    i = pl.program_id(1)
    nl = pl.num_programs(1)
    tl = main_ref.shape[0]
    hr = prev_ref.shape[0]
    pv = jnp.where(i > 0, prev_ref[...].astype(F32), 0.0)
    nx = jnp.where(i < nl - 1, next_ref[...].astype(F32), 0.0)
    ext = jnp.concatenate([pv, main_ref[...].astype(F32), nx], axis=0)
    n = ext.shape[0]
    acc = None
    for k in range(SSD_CONV):
        off = k - SSD_CONV // 2
        sh = (-off) % n
        term = (ext if sh == 0 else pltpu.roll(ext, sh, 0))[hr:hr + tl] * w_ref[k:k + 1, :]
        acc = term if acc is None else acc + term
    acc = acc + b_ref[...]
    o_ref[...] = (acc * (1.0 / (1.0 + jnp.exp(-acc)))).astype(o_ref.dtype)


def _ssd_conv(p_main, conv_w, conv_b, Bsz, L, *, tl=512):
    T = p_main.shape[0]
    W = p_main.shape[1] // 4
    tl = min(tl, L)
    nl = L // tl
    hr = BF16_SUBLANES
    lh = L // hr
    th = tl // hr
    return pl.pallas_call(
        _conv_kernel, grid=(Bsz, nl, 2),
        in_specs=[
            pl.BlockSpec((hr, W), lambda b, i, c: (jnp.maximum(b * lh + i * th - 1, b * lh), 1 + c)),
            pl.BlockSpec((tl, W), lambda b, i, c: (b * nl + i, 1 + c)),
            pl.BlockSpec((hr, W), lambda b, i, c: (jnp.minimum(b * lh + (i + 1) * th, (b + 1) * lh - 1), 1 + c)),
            pl.BlockSpec((SSD_CONV, W), lambda b, i, c: (0, c)),
            pl.BlockSpec((1, W), lambda b, i, c: (0, c)),
        ],
        out_specs=pl.BlockSpec((tl, W), lambda b, i, c: (b * nl + i, c)),
        out_shape=jax.ShapeDtypeStruct((T, 2 * W), BF16),
        compiler_params=_params("parallel", "parallel", "parallel"), name="ssd_conv",
    )(p_main, p_main, p_main, conv_w, conv_b.reshape(1, -1))


def _split3(v):
    c1 = v.astype(BF16)
    r1 = v - c1.astype(F32)
    c2 = r1.astype(BF16)
    c3 = (r1 - c2.astype(F32)).astype(BF16)
    return jnp.concatenate([c1, c2, c3], axis=1)


def _scan_kernel(xbc_ref, dt_ref, tri_ref, alog_ref, dtb_ref, eh_ref, es_ref, h0_ref, y_ref, h_ref, *, width):
    c = pl.program_id(2)

    @pl.when(c == 0)
    def _():
        h_ref[...] = h0_ref[...]

    n_heads = width // SSD_HEAD_DIM
    hpg = n_heads // SSD_GROUPS
    gw = hpg * SSD_HEAD_DIM
    gn = SSD_GROUPS * SSD_STATE
    raw = dt_ref[0] + dtb_ref[0]
    dt = jnp.maximum(raw, 0.0) + jnp.log1p(jnp.exp(-jnp.abs(raw)))
    da = dt * (-jnp.exp(alog_ref[0]))
    tri = tri_ref[0]
    cum = jnp.dot(tri, da, precision=HIGHEST, preferred_element_type=F32)
    cum_t = lax.dot_general(da, tri, (((0,), (1,)), ((), ())),
                            precision=HIGHEST, preferred_element_type=F32)
    tot = jnp.broadcast_to(jnp.sum(da, axis=0, keepdims=True), (8, n_heads))
    cum3 = _split3(cum)
    cum_hd = jnp.dot(cum3, eh_ref[...], preferred_element_type=F32)
    dt_hd = jnp.dot(_split3(dt), eh_ref[...], preferred_element_type=F32)
    tot_hd = jnp.dot(_split3(tot), eh_ref[...], preferred_element_type=F32)[0:1]
    cum_st = jnp.dot(cum3, es_ref[...], preferred_element_type=F32)
    e_cum = jnp.exp(cum_hd)
    e_tot = jnp.exp(tot_hd)
    xdt = xbc_ref[:, :width].astype(F32) * dt_hd
    xdt_b = xdt.astype(BF16)
    xdt_end = (xdt * jnp.exp(tot_hd - cum_hd)).astype(BF16)
    mask = tri > 0.0
    first_head = lax.broadcasted_iota(jnp.int32, (SSD_CHUNK, 2 * SSD_HEAD_DIM), 1) < SSD_HEAD_DIM
    zero = jnp.zeros((), BF16)
    for g in range(SSD_GROUPS):
        bg = xbc_ref[:, width + g * SSD_STATE: width + (g + 1) * SSD_STATE]
        cg = xbc_ref[:, width + gn + g * SSD_STATE: width + gn + (g + 1) * SSD_STATE]
        gm = lax.dot_general(cg, bg, (((1,), (1,)), ((), ())), preferred_element_type=F32)
        h_in = h_ref[0, 0, g]
        y_off = jnp.dot(cg, h_in.astype(BF16), preferred_element_type=F32)
        for pair in range(hpg // 2):
            scores = []
            for hh in (2 * pair, 2 * pair + 1):
                h = g * hpg + hh
                seg = cum_st[:, h * SSD_CHUNK:(h + 1) * SSD_CHUNK] - cum_t[h:h + 1, :]
                scores.append((gm * jnp.exp(jnp.where(mask, seg, -jnp.inf))).astype(BF16))
            c0 = g * gw + pair * 2 * SSD_HEAD_DIM
            xp = xdt_b[:, c0:c0 + 2 * SSD_HEAD_DIM]
            rhs = jnp.concatenate([jnp.where(first_head, xp, zero), jnp.where(first_head, zero, xp)], axis=0)
            yp = jnp.dot(jnp.concatenate(scores, axis=1), rhs, preferred_element_type=F32)
            yp = yp + y_off[:, pair * 2 * SSD_HEAD_DIM:(pair + 1) * 2 * SSD_HEAD_DIM] * e_cum[:, c0:c0 + 2 * SSD_HEAD_DIM]
            y_ref[0, :, c0:c0 + 2 * SSD_HEAD_DIM] = yp.astype(y_ref.dtype)
        st = lax.dot_general(bg, xdt_end[:, g * gw:(g + 1) * gw], (((0,), (0,)), ((), ())),
                             preferred_element_type=F32)
        h_ref[0, 0, g] = h_in * e_tot[:, g * gw:(g + 1) * gw] + st


def _head_spread(n_heads, per_head):
    e = np.kron(np.eye(n_heads, dtype=np.float32), np.ones((1, per_head), np.float32))
    return jnp.asarray(np.concatenate([e, e, e], axis=0), dtype=BF16)


def _ssd_scan(xbc, dt2, tri, a_log, dt_bias, h0, Bsz, L, width):
    T = xbc.shape[0]
    H = width // SSD_HEAD_DIM
    nc = L // SSD_CHUNK

    def rows(d, b, c):
        return b * nc + c + d * (nc - 1 - 2 * c)

    st = pl.BlockSpec((1, 1) + h0.shape[2:], lambda d, b, c: (d, b, 0, 0, 0))
    vec = pl.BlockSpec((1, 1, H), lambda d, b, c: (d, 0, 0))
    return pl.pallas_call(
        functools.partial(_scan_kernel, width=width), grid=(2, Bsz, nc),
        in_specs=[
            pl.BlockSpec((SSD_CHUNK, 2 * width), lambda d, b, c: (rows(d, b, c), 0)),
            pl.BlockSpec((1, SSD_CHUNK, H), lambda d, b, c: (d, rows(d, b, c), 0)),
            pl.BlockSpec((1, SSD_CHUNK, SSD_CHUNK), lambda d, b, c: (d, 0, 0)),
            vec, vec,
            pl.BlockSpec((3 * H, width), lambda d, b, c: (0, 0)),
            pl.BlockSpec((3 * H, H * SSD_CHUNK), lambda d, b, c: (0, 0)),
            st,
        ],
        out_specs=[pl.BlockSpec((1, SSD_CHUNK, width), lambda d, b, c: (d, rows(d, b, c), 0)), st],
        out_shape=[jax.ShapeDtypeStruct((2, T, width), BF16),
                   jax.ShapeDtypeStruct(h0.shape, F32)],
        compiler_params=_params("parallel", "parallel", "arbitrary"), name="ssd_scan",
    )(xbc, dt2, tri, a_log.reshape(2, 1, H), dt_bias.reshape(2, 1, H),
      _head_spread(H, SSD_HEAD_DIM), _head_spread(H, SSD_CHUNK), h0)


def _ssd_out_kernel(y_ref, x_ref, z_ref, d_ref, nw_ref, o_ref):
    xs = x_ref[...].astype(F32)
    z = z_ref[...].astype(F32)
    y = y_ref[0].astype(F32) + y_ref[1].astype(F32) + xs * d_ref[...]
    y = y * (z * (1.0 / (1.0 + jnp.exp(-z))))
    gw = y.shape[1] // SSD_GROUPS
    for g in range(SSD_GROUPS):
        seg = y[:, g * gw:(g + 1) * gw]
        ms = jnp.mean(seg * seg, axis=-1, keepdims=True)
        o_ref[:, g * gw:(g + 1) * gw] = (seg * lax.rsqrt(ms + EPS) * nw_ref[:, g * gw:(g + 1) * gw]
                                         ).astype(o_ref.dtype)


def _ssd_out(y, xbc, p_main, d_skip, norm_w, *, tl=512):
    T = xbc.shape[0]
    W = xbc.shape[1] // 2
    tl = min(tl, T)
    row = lambda i: (i, 0)
    one = pl.BlockSpec((1, W), lambda i: (0, 0))
    return pl.pallas_call(
        _ssd_out_kernel, grid=(T // tl,),
        in_specs=[pl.BlockSpec((2, tl, W), lambda i: (0, i, 0)),
                  pl.BlockSpec((tl, W), row), pl.BlockSpec((tl, W), row), one, one],
        out_specs=pl.BlockSpec((tl, W), row),
        out_shape=jax.ShapeDtypeStruct((T, W), BF16),
        compiler_params=_params("parallel"), name="ssd_out",
    )(y, xbc, p_main, jnp.repeat(d_skip, SSD_HEAD_DIM).reshape(1, W), norm_w.reshape(1, W))


def _pool_kernel(prev_ref, v_ref, next_ref, band_ref, pw_ref, pb_ref, ps_ref, o_ref, *, rows):
    i = pl.program_id(1)
    nl = pl.num_programs(1)
    tl = v_ref.shape[0]
    gw = v_ref.shape[1] // len(POOL_WINDOWS)
    tile = band_ref.shape[1]
    pos = lax.broadcasted_iota(jnp.int32, (tl, 128), 0)
    r = i * (tl // GRID_W) + lax.shift_right_logical(pos, GRID_W.bit_length() - 1)
    cidx = lax.bitwise_and(pos, GRID_W - 1)
    for g, w in enumerate(POOL_WINDOWS):
        lo, hi = -(w // 2), w - 1 - w // 2
        cols = slice(g * gw, (g + 1) * gw)
        pv = jnp.where(i > 0, prev_ref[:, cols], jnp.zeros((), BF16))
        nx = jnp.where(i < nl - 1, next_ref[:, cols], jnp.zeros((), BF16))
        vg = v_ref[:, cols]
        ext = jnp.concatenate([pv, vg, nx], axis=0)
        band = band_ref[g]
        cs = jnp.concatenate(
            [jnp.dot(band, ext[t * tile:(t + 1) * tile], preferred_element_type=F32)
             for t in range(ext.shape[0] // tile)], axis=0)
        rs = None
        for o in range(lo, hi + 1):
            s0 = POOL_HALO + GRID_W * o
            term = cs[s0:s0 + tl]
            rs = term if rs is None else rs + term
        n_r = jnp.minimum(r + hi, rows - 1) - jnp.maximum(r + lo, 0) + 1
        n_c = jnp.minimum(cidx + hi, GRID_W - 1) - jnp.maximum(cidx + lo, 0) + 1
        inv = 1.0 / (n_r * n_c).astype(F32)
        inv = jnp.concatenate([inv] * (gw // 128), axis=1)
        d = rs * inv - vg.astype(F32)
        yv = jnp.dot(d.astype(BF16), pw_ref[g], preferred_element_type=F32) + pb_ref[:, cols]
        o_ref[:, cols] = (yv * ps_ref[:, cols]).astype(o_ref.dtype)


def _pool_bands():
    t = 2 * GRID_W
    i = np.arange(t)[:, None]
    j = np.arange(t)[None, :]
    out = []
    for w in POOL_WINDOWS:
        lo, hi = -(w // 2), w - 1 - w // 2
        dc = j % GRID_W - i % GRID_W
        out.append(((i // GRID_W == j // GRID_W) & (dc >= lo) & (dc <= hi)).astype(np.float32))
    return jnp.asarray(np.stack(out), dtype=BF16)


def _pool_mixer(p_main, pool_w, pool_b, pool_scale, Bsz, L, *, tl=1024):
    T = p_main.shape[0]
    W = p_main.shape[1] // 4
    tl = min(tl, L)
    assert L % tl == 0 and tl % POOL_HALO == 0
    nl = L // tl
    lh = L // POOL_HALO
    th = tl // POOL_HALO
    cb = 3
    full = lambda shape: pl.BlockSpec(shape, lambda b, i: (0,) * len(shape))
    return pl.pallas_call(
        functools.partial(_pool_kernel, rows=L // GRID_W), grid=(Bsz, nl),
        in_specs=[
            pl.BlockSpec((POOL_HALO, W), lambda b, i: (jnp.maximum(b * lh + i * th - 1, b * lh), cb)),
            pl.BlockSpec((tl, W), lambda b, i: (b * nl + i, cb)),
            pl.BlockSpec((POOL_HALO, W), lambda b, i: (jnp.minimum(b * lh + (i + 1) * th, (b + 1) * lh - 1), cb)),
            full((len(POOL_WINDOWS), 2 * GRID_W, 2 * GRID_W)),
            full(pool_w.shape), full((1, W)), full((1, W)),
        ],
        out_specs=pl.BlockSpec((tl, W), lambda b, i: (b * nl + i, 0)),
        out_shape=jax.ShapeDtypeStruct((T, W), BF16),
        compiler_params=_params("parallel", "parallel"), name="pool_mixer",
    )(p_main, p_main, p_main, _pool_bands(), pool_w.astype(BF16),
      pool_b.reshape(1, W), pool_scale.reshape(1, W))


def _gmlp_kernel(u_ref, v_ref, g_ref, b_ref, ws_ref, bs_ref, o_ref):
    v = v_ref[...].astype(F32)
    mu = jnp.mean(v, axis=-1, keepdims=True)
    vc = v - mu
    var = jnp.mean(vc * vc, axis=-1, keepdims=True)
    vn = (vc * lax.rsqrt(var + EPS) * g_ref[...] + b_ref[...]).astype(BF16)
    tl, W = v.shape
    hd = W // CG_HEADS
    for n in range(tl // CG_CHUNK):
        rs = slice(n * CG_CHUNK, (n + 1) * CG_CHUNK)
        for h in range(CG_HEADS):
            cs = slice(h * hd, (h + 1) * hd)
            sv = jnp.dot(ws_ref[h], vn[rs, cs], preferred_element_type=F32) + bs_ref[:, h:h + 1]
            o_ref[rs, cs] = (u_ref[rs, cs].astype(F32) * sv).astype(o_ref.dtype)


def _gmlp_gate(uv, ln_g, ln_b, ws, bs, *, tl=256):
    T = uv.shape[0]
    W = uv.shape[1] // 2
    tl = min(tl, T)
    one = pl.BlockSpec((1, W), lambda i: (0, 0))
    return pl.pallas_call(
        _gmlp_kernel, grid=(T // tl,),
        in_specs=[pl.BlockSpec((tl, W), lambda i: (i, 0)), pl.BlockSpec((tl, W), lambda i: (i, 1)),
                  one, one,
                  pl.BlockSpec(ws.shape, lambda i: (0, 0, 0)),
                  pl.BlockSpec((CG_CHUNK, CG_HEADS), lambda i: (0, 0))],
        out_specs=pl.BlockSpec((tl, W), lambda i: (i, 0)),
        out_shape=jax.ShapeDtypeStruct((T, W), BF16),
        compiler_params=_params("parallel"), name="gmlp_gate",
    )(uv, uv, ln_g.reshape(1, W), ln_b.reshape(1, W), ws.astype(BF16), jnp.transpose(bs))


def _gather_kernel(nu_ref, tok_ref, f_hbm, o_ref, buf, sem):
    tb = o_ref.shape[0]

    def row_copy(t):
        return pltpu.make_async_copy(f_hbm.at[pl.ds(tok_ref[0, 0, t], 1), :], buf.at[pl.ds(t, 1), :], sem)

    def issue(t, carry):
        row_copy(t).start()
        return carry

    def drain(t, carry):
        row_copy(t).wait()
        return carry

    @pl.when(pl.program_id(0) < nu_ref[0])
    def _():
        lax.fori_loop(0, tb, issue, 0)
        lax.fori_loop(0, tb, drain, 0)
        o_ref[...] = buf[...].astype(o_ref.dtype)


def _moe_gather(f, row_tok, n_used):
    T, D = f.shape
    P = row_tok.shape[0]
    nb = P // MOE_TM
    grid_spec = pltpu.PrefetchScalarGridSpec(
        num_scalar_prefetch=1, grid=(nb,),
        in_specs=[pl.BlockSpec((1, 1, MOE_TM), lambda i, nu: (i, 0, 0), memory_space=pltpu.SMEM),
                  pl.BlockSpec(memory_space=pl.ANY)],
        out_specs=pl.BlockSpec((MOE_TM, D), lambda i, nu: (i, 0)),
        scratch_shapes=[pltpu.VMEM((MOE_TM, D), F32), pltpu.SemaphoreType.DMA(())])
    return pl.pallas_call(
        _gather_kernel, grid_spec=grid_spec,
        out_shape=jax.ShapeDtypeStruct((P, D), BF16),
        compiler_params=_params("arbitrary"), name="moe_gather",
    )(n_used, row_tok.reshape(nb, 1, MOE_TM), f)


def _expert_changed(be_ref):
    i = pl.program_id(1)
    return jnp.logical_or(i == 0, be_ref[i] != be_ref[jnp.maximum(i - 1, 0)])


def _gu_kernel(be_ref, nu_ref, x_ref, wg_ref, wu_ref, bg_ref, bu_ref, o_ref, wg_bf, wu_bf):
    @pl.when(pl.program_id(1) < nu_ref[0])
    def _():
        @pl.when(_expert_changed(be_ref))
        def _():
            wg_bf[...] = wg_ref[0].astype(BF16)
            wu_bf[...] = wu_ref[0].astype(BF16)

        xv = x_ref[...]
        gv = jnp.dot(xv, wg_bf[...], preferred_element_type=F32) + bg_ref[0]
        uv = jnp.dot(xv, wu_bf[...], preferred_element_type=F32) + bu_ref[0]
        gv = jnp.minimum(gv, SWIGLU_LIMIT)
        uv = jnp.clip(uv, -SWIGLU_LIMIT, SWIGLU_LIMIT)
        act = gv * (1.0 / (1.0 + jnp.exp(-SWIGLU_ALPHA * gv))) * (uv + 1.0)
        o_ref[...] = act.astype(o_ref.dtype)


def _moe_gate_up(xs, block_e, n_used, w_gu, b_gu, *, tn=512):
    P, D = xs.shape
    F = w_gu.shape[2] // 2
    nj = F // tn
    nb = P // MOE_TM
    b3 = b_gu.reshape(N_EXPERTS, 1, 2 * F)
    grid_spec = pltpu.PrefetchScalarGridSpec(
        num_scalar_prefetch=2, grid=(nj, nb),
        in_specs=[pl.BlockSpec((MOE_TM, D), lambda j, i, be, nu: (i, 0)),
                  pl.BlockSpec((1, D, tn), lambda j, i, be, nu: (be[i], 0, j)),
                  pl.BlockSpec((1, D, tn), lambda j, i, be, nu: (be[i], 0, nj + j)),
                  pl.BlockSpec((1, 1, tn), lambda j, i, be, nu: (be[i], 0, j)),
                  pl.BlockSpec((1, 1, tn), lambda j, i, be, nu: (be[i], 0, nj + j))],
        out_specs=pl.BlockSpec((MOE_TM, tn), lambda j, i, be, nu: (i, j)),
        scratch_shapes=[pltpu.VMEM((D, tn), BF16), pltpu.VMEM((D, tn), BF16)])
    return pl.pallas_call(
        _gu_kernel, grid_spec=grid_spec,
        out_shape=jax.ShapeDtypeStruct((P, F), BF16),
        compiler_params=_params("arbitrary", "arbitrary"), name="moe_gate_up",
    )(block_e, n_used, xs, w_gu, w_gu, b3, b3)


def _down_kernel(be_ref, nu_ref, a_ref, w_ref, b_ref, o_ref, w_bf):
    @pl.when(pl.program_id(1) < nu_ref[0])
    def _():
        @pl.when(_expert_changed(be_ref))
        def _():
            w_bf[...] = w_ref[0].astype(BF16)

        o_ref[...] = jnp.dot(a_ref[...], w_bf[...], preferred_element_type=F32) + b_ref[0]


def _moe_down(act, block_e, n_used, w_down, b_down, *, tn=512):
    P, F = act.shape
    D = w_down.shape[2]
    nb = P // MOE_TM
    grid_spec = pltpu.PrefetchScalarGridSpec(
        num_scalar_prefetch=2, grid=(D // tn, nb),
        in_specs=[pl.BlockSpec((MOE_TM, F), lambda j, i, be, nu: (i, 0)),
                  pl.BlockSpec((1, F, tn), lambda j, i, be, nu: (be[i], 0, j)),
                  pl.BlockSpec((1, 1, tn), lambda j, i, be, nu: (be[i], 0, j))],
        out_specs=pl.BlockSpec((MOE_TM, tn), lambda j, i, be, nu: (i, j)),
        scratch_shapes=[pltpu.VMEM((F, tn), BF16)])
    return pl.pallas_call(
        _down_kernel, grid_spec=grid_spec,
        out_shape=jax.ShapeDtypeStruct((P, D), F32),
        compiler_params=_params("arbitrary", "arbitrary"), name="moe_down",
    )(block_e, n_used, act, w_down, b_down.reshape(N_EXPERTS, 1, D))


def _combine_kernel(dest_ref, gt_ref, x_ref, g_ref, sc_ref, sh_ref, ys_hbm, *rest, want_x):
    if want_x:
        xo_ref, h_ref, buf, sem = rest
    else:
        h_ref, buf, sem = rest
    tb = x_ref.shape[0]

    def row_copy(t, k):
        return pltpu.make_async_copy(ys_hbm.at[pl.ds(dest_ref[0, 0, t * TOP_K + k], 1), :],
                                     buf.at[k, pl.ds(t, 1), :], sem)

    def issue(t, carry):
        for k in range(TOP_K):
            row_copy(t, k).start()
        return carry

    def drain(t, carry):
        for k in range(TOP_K):
            row_copy(t, k).wait()
        return carry

    lax.fori_loop(0, tb, issue, 0)
    lax.fori_loop(0, tb, drain, 0)
    moe = None
    for k in range(TOP_K):
        term = buf[k] * gt_ref[:, k:k + 1]
        moe = term if moe is None else moe + term
    xv = x_ref[...] + g_ref[0] * moe
    if want_x:
        xo_ref[...] = xv
    h_ref[...] = _rms_mod(xv, sc_ref[0], sh_ref[0]).astype(h_ref.dtype)


def _moe_combine(ys, dest, gates, x2, g, sc, sh, L, *, want_x, out_dtype, tb=256):
    T, D = x2.shape
    tb = min(tb, L)
    nl = L // tb
    mod = pl.BlockSpec((1, 1, D), lambda i: (i // nl, 0, 0))
    row = pl.BlockSpec((tb, D), lambda i: (i, 0))
    out_specs = [row, row] if want_x else [row]
    out_shape = ([jax.ShapeDtypeStruct((T, D), F32)] if want_x else []) + [jax.ShapeDtypeStruct((T, D), out_dtype)]
    return pl.pallas_call(
        functools.partial(_combine_kernel, want_x=want_x), grid=(T // tb,),
        in_specs=[pl.BlockSpec((1, 1, tb * TOP_K), lambda i: (i, 0, 0), memory_space=pltpu.SMEM),
                  pl.BlockSpec((tb, TOP_K), lambda i: (i, 0)),
                  row, mod, mod, mod, pl.BlockSpec(memory_space=pl.ANY)],
        out_specs=out_specs, out_shape=out_shape,
        scratch_shapes=[pltpu.VMEM((TOP_K, tb, D), F32), pltpu.SemaphoreType.DMA(())],
        compiler_params=_params("arbitrary"), name="moe_combine",
    )(dest.reshape(T // tb, 1, tb * TOP_K), gates, x2, g, sc, sh, ys)


def _moe_routing(top_i):
    T = top_i.shape[0]
    A = T * TOP_K
    P = A + N_EXPERTS * MOE_TM
    nb = P // MOE_TM
    flat_e = top_i.reshape(A)
    onehot = (flat_e[:, None] == jnp.arange(N_EXPERTS, dtype=jnp.int32)[None, :]).astype(jnp.int32)
    cs = jnp.cumsum(onehot, axis=0)
    rank = jnp.take_along_axis(cs, flat_e[:, None], axis=1)[:, 0] - 1
    counts = cs[-1]
    padded = (counts + MOE_TM - 1) // MOE_TM * MOE_TM
    pad_end = jnp.cumsum(padded)
    pad_start = pad_end - padded
    dest = (pad_start[flat_e] + rank).astype(jnp.int32)
    block_e = jnp.minimum(jnp.searchsorted(pad_end, jnp.arange(nb, dtype=jnp.int32) * MOE_TM, side="right"),
                          N_EXPERTS - 1).astype(jnp.int32)
    n_used = (pad_end[-1:] // MOE_TM).astype(jnp.int32)
    _, sorted_tok = lax.sort((dest, jnp.arange(A, dtype=jnp.int32) // TOP_K), num_keys=1)
    start = jnp.cumsum(counts) - counts
    row = jnp.arange(P, dtype=jnp.int32)
    row_e = jnp.repeat(block_e, MOE_TM)
    off = row - pad_start[row_e]
    src = jnp.clip(start[row_e] + off, 0, A - 1)
    row_tok = jnp.where(off < counts[row_e], sorted_tok[src], 0).astype(jnp.int32)
    return dest, row_tok, block_e, n_used


def _moe(x2, m, g1, sc2, sh2, g2, nsc, nsh, router_w, router_b, w_gu, b_gu, w_down, b_down, L,
         *, want_x, out_dtype):
    xo, f, top_i, gates = _res_norm_router(x2, m, g1, sc2, sh2, jnp.transpose(router_w),
                                           router_b.reshape(N_EXPERTS, 1), L)
    dest, row_tok, block_e, n_used = _moe_routing(top_i)
    xs = _moe_gather(f, row_tok, n_used)
    act = _moe_gate_up(xs, block_e, n_used, w_gu, b_gu)
    ys = _moe_down(act, block_e, n_used, w_down, b_down)
    return _moe_combine(ys, dest, gates, xo, g2, nsc, nsh, L, want_x=want_x, out_dtype=out_dtype)


def _ssd_tri():
    l = np.arange(SSD_CHUNK)[:, None]
    j = np.arange(SSD_CHUNK)[None, :]
    return jnp.asarray(np.stack([(j <= l), (j >= l)]).astype(np.float32))


def _ssd_stream(h_bf, w_main, w_dt, conv_w, conv_b, a_log, dt_bias, h0, Bsz, L, width):
    T = h_bf.shape[0]
    H = width // SSD_HEAD_DIM
    p_main = _matmul([(h_bf, w_main)], tm=1024, tn=1024, out_dtype=BF16, name="even_in_proj")
    dt_raw = _matmul([(h_bf, w_dt)], tm=512, tn=128, out_dtype=F32, precision=HIGHEST, name="dt_proj")
    dt2 = jnp.transpose(dt_raw[:, :2 * H].reshape(T, 2, H), (1, 0, 2))
    xbc = _ssd_conv(p_main, conv_w, conv_b, Bsz, L)
    y, h_last = _ssd_scan(xbc, dt2, _ssd_tri(), a_log, dt_bias, h0, Bsz, L, width)
    return p_main, xbc, y, h_last


def kernel(x, c, ctx, c_ctx, ada_w, ada_b, even_in_w, ssd_conv_w, ssd_conv_b, ssd_a_log, ssd_dt_bias,
           ssd_d, ssd_norm_w, pool_w, pool_b, pool_scale, even_out_w, cg_in_w, cg_ln_g, cg_ln_b, cg_ws,
           cg_bs, cg_out_w, router_w, router_b, moe_w_gu, moe_b_gu, moe_w_down, moe_b_down, final_norm_w):
    Bsz, S, D = x.shape
    Lc = ctx.shape[1]
    T = Bsz * S
    width = D
    H = width // SSD_HEAD_DIM
    gn = SSD_GROUPS * SSD_STATE
    x2 = x.reshape(T, D)
    ctx2 = ctx.reshape(Bsz * Lc, D)

    cin = jnp.zeros((8, D), F32).at[:Bsz].set(c).at[Bsz].set(c_ctx)
    mods = [_matmul([(cin, ada_w[i])], tm=8, tn=1024, out_dtype=F32, bias=ada_b[i], pre_act="silu",
                    precision=HIGHEST, name="ada_mod") for i in range(DEPTH)]

    def lat_mods(i):
        return [mods[i][:Bsz, k * D:(k + 1) * D].reshape(Bsz, 1, D) for k in range(6)]

    sh1, sc1, g1, sh2, sc2, g2 = lat_mods(0)
    sh_c = jnp.broadcast_to(mods[0][Bsz, :D].reshape(1, 1, D), (Bsz, 1, D))
    sc_c = jnp.broadcast_to(mods[0][Bsz, D:2 * D].reshape(1, 1, D), (Bsz, 1, D))
    w_in = even_in_w[0]
    o_dt = 2 * width + 2 * gn
    w_main = jnp.concatenate([w_in[:, :o_dt], w_in[:, o_dt + 2 * H:]], axis=1).astype(BF16)
    w_dt = jnp.pad(w_in[:, o_dt:o_dt + 2 * H], ((0, 0), (0, 128 - 2 * H)))
    ssd_p = (w_main, w_dt, ssd_conv_w[0], ssd_conv_b[0], ssd_a_log[0], ssd_dt_bias[0])

    h_c = _norm_mod(ctx2, sc_c, sh_c, Lc)
    h0 = jnp.zeros((2, Bsz, SSD_GROUPS, SSD_STATE, width // SSD_GROUPS), F32)
    _, _, _, ctx_state = _ssd_stream(h_c, *ssd_p, h0, Bsz, Lc, width)

    h_l = _norm_mod(x2, sc1, sh1, S)
    p_main, xbc, y, _ = _ssd_stream(h_l, *ssd_p, ctx_state, Bsz, S, width)
    y_ssd = _ssd_out(y, xbc, p_main, ssd_d[0], ssd_norm_w[0])
    y_pool = _pool_mixer(p_main, pool_w[0], pool_b[0], pool_scale[0], Bsz, S)
    w_out = even_out_w[0].astype(BF16)
    m = _matmul([(y_ssd, w_out[:width]), (y_pool, w_out[width:])], tm=1024, tn=1024, out_dtype=F32,
                name="even_out_proj")

    nsh1, nsc1, ng1, nsh2, nsc2, ng2 = lat_mods(1)
    x2, h_l = _moe(x2, m, g1, sc2, sh2, g2, nsc1, nsh1, router_w[0], router_b[0], moe_w_gu[0], moe_b_gu[0],
                   moe_w_down[0], moe_b_down[0], S, want_x=True, out_dtype=BF16)

    uv = _matmul([(h_l, cg_in_w[0].astype(BF16))], tm=1024, tn=1024, out_dtype=BF16, act="gelu",
                 name="cg_in_proj")
    gated = _gmlp_gate(uv, cg_ln_g[0], cg_ln_b[0], cg_ws[0], cg_bs[0])
    m = _matmul([(gated, cg_out_w[0].astype(BF16))], tm=1024, tn=1024, out_dtype=F32, name="cg_out_proj")
    fin_sc = jnp.broadcast_to((final_norm_w - 1.0).reshape(1, 1, D), (Bsz, 1, D))
    fin_sh = jnp.zeros((Bsz, 1, D), F32)
    (out,) = _moe(x2, m, ng1, nsc2, nsh2, ng2, fin_sc, fin_sh, router_w[1], router_b[1], moe_w_gu[1],
                  moe_b_gu[1], moe_w_down[1], moe_b_down[1], S, want_x=False, out_dtype=F32)
    return out.reshape(Bsz, S, D)
```

```python
import functools
import math

import numpy as np
import jax
import jax.numpy as jnp
from jax import lax
from jax.experimental import pallas as pl
from jax.experimental.pallas import tpu as pltpu

F32 = jnp.float32
BF16 = jnp.bfloat16
HIGHEST = lax.Precision.HIGHEST

GRID_W = 64
EPS = 1e-6
DEPTH = 2

SSD_HEAD_DIM = 64
SSD_GROUPS = 8
SSD_STATE = 128
SSD_CONV = 4
SSD_CHUNK = 128

POOL_WINDOWS = (2, 4, 8, 16)
POOL_HALO = 512

CG_HEADS = 8
CG_CHUNK = 128

N_EXPERTS = 32
TOP_K = 4
SWIGLU_ALPHA = 1.702
SWIGLU_LIMIT = 7.0
MOE_TM = 512
DMA_UNROLL = 8

V7X_VMEM_LIMIT_BYTES = 56 * 1024 * 1024
BF16_SUBLANES = 16


def _params(*sem):
    return pltpu.CompilerParams(dimension_semantics=sem, vmem_limit_bytes=V7X_VMEM_LIMIT_BYTES)


def _mm_kernel(*refs, n_pairs, has_bias, pre_act, act, precision):
    o_ref = refs[-1]
    acc = None
    for p in range(n_pairs):
        xv = refs[2 * p][...]
        wv = refs[2 * p + 1][...]
        if precision is not None:
            xv = xv.astype(F32)
        if pre_act == "silu":
            xv = xv * (1.0 / (1.0 + jnp.exp(-xv)))
        d = jnp.dot(xv, wv, preferred_element_type=F32, precision=precision)
        acc = d if acc is None else acc + d
    if has_bias:
        acc = acc + refs[2 * n_pairs][...]
    if act == "gelu":
        acc = 0.5 * acc * (1.0 + lax.erf(acc * (1.0 / math.sqrt(2.0))))
    o_ref[...] = acc.astype(o_ref.dtype)


def _matmul(pairs, *, tm, tn, out_dtype, name, bias=None, pre_act=None, act=None, precision=None):
    M = pairs[0][0].shape[0]
    N = pairs[0][1].shape[1]
    tm = min(tm, M)
    tn = min(tn, N)
    assert M % tm == 0 and N % tn == 0
    in_specs, args = [], []
    for xa, wa in pairs:
        K = xa.shape[1]
        in_specs += [pl.BlockSpec((tm, K), lambda i, j: (i, 0)),
                     pl.BlockSpec((K, tn), lambda i, j: (0, j))]
        args += [xa, wa]
    if bias is not None:
        in_specs.append(pl.BlockSpec((1, tn), lambda i, j: (0, j)))
        args.append(bias.reshape(1, N))
    kern = functools.partial(_mm_kernel, n_pairs=len(pairs), has_bias=bias is not None,
                             pre_act=pre_act, act=act, precision=precision)
    return pl.pallas_call(
        kern, grid=(M // tm, N // tn), in_specs=in_specs,
        out_specs=pl.BlockSpec((tm, tn), lambda i, j: (i, j)),
        out_shape=jax.ShapeDtypeStruct((M, N), out_dtype),
        compiler_params=_params("parallel", "parallel"), name=name)(*args)


def _rms_mod(xv, sc, sh):
    ms = jnp.mean(xv * xv, axis=-1, keepdims=True)
    return xv * lax.rsqrt(ms + EPS) * (1.0 + sc) + sh


def _norm_kernel(x_ref, sc_ref, sh_ref, h_ref):
    h_ref[...] = _rms_mod(x_ref[...], sc_ref[0], sh_ref[0]).astype(h_ref.dtype)


def _norm_mod(x2, sc, sh, L, *, tl=512):
    T, D = x2.shape
    tl = min(tl, L)
    nl = L // tl
    mod = pl.BlockSpec((1, 1, D), lambda i: (i // nl, 0, 0))
    return pl.pallas_call(
        _norm_kernel, grid=(T // tl,),
        in_specs=[pl.BlockSpec((tl, D), lambda i: (i, 0)), mod, mod],
        out_specs=pl.BlockSpec((tl, D), lambda i: (i, 0)),
        out_shape=jax.ShapeDtypeStruct((T, D), BF16),
        compiler_params=_params("parallel"), name="norm_mod")(x2, sc, sh)


def _res_router_kernel(x_ref, m_ref, g_ref, sc_ref, sh_ref, rw_ref, rb_ref,
                       xo_ref, f_ref, ti_ref, tg_ref):
    xv = x_ref[...] + g_ref[0] * m_ref[...]
    xo_ref[...] = xv
    f = _rms_mod(xv, sc_ref[0], sh_ref[0])
    f_ref[...] = f
    lg = lax.dot_general(rw_ref[...], f, (((1,), (1,)), ((), ())),
                         precision=HIGHEST, preferred_element_type=F32) + rb_ref[...]
    iota = lax.broadcasted_iota(jnp.int32, lg.shape, 0)
    vals, idxs = [], []
    for _ in range(TOP_K):
        mx = jnp.max(lg, axis=0, keepdims=True)
        ix = jnp.min(jnp.where(lg == mx, iota, N_EXPERTS), axis=0, keepdims=True)
        vals.append(mx)
        idxs.append(ix)
        lg = jnp.where(iota == ix, -jnp.inf, lg)
    tv = jnp.concatenate(vals, axis=0)
    ex = jnp.exp(tv - vals[0])
    ti_ref[0] = jnp.concatenate(idxs, axis=0)
    tg_ref[0] = ex / jnp.sum(ex, axis=0, keepdims=True)


def _res_norm_router(x2, m, g, sc, sh, rw_t, rb, L, *, tl=512):
    T, D = x2.shape
    tl = min(tl, L)
    nl = L // tl
    nb = T // tl
    mod = pl.BlockSpec((1, 1, D), lambda i: (i // nl, 0, 0))
    row = pl.BlockSpec((tl, D), lambda i: (i, 0))
    tk = pl.BlockSpec((1, TOP_K, tl), lambda i: (i, 0, 0))
    xo, f, ti, tg = pl.pallas_call(
        _res_router_kernel, grid=(nb,),
        in_specs=[row, row, mod, mod, mod,
                  pl.BlockSpec((N_EXPERTS, D), lambda i: (0, 0)),
                  pl.BlockSpec((N_EXPERTS, 1), lambda i: (0, 0))],
        out_specs=[row, row, tk, tk],
        out_shape=[jax.ShapeDtypeStruct((T, D), F32), jax.ShapeDtypeStruct((T, D), F32),
                   jax.ShapeDtypeStruct((nb, TOP_K, tl), jnp.int32),
                   jax.ShapeDtypeStruct((nb, TOP_K, tl), F32)],
        compiler_params=_params("parallel"), name="res_norm_router")(x2, m, g, sc, sh, rw_t, rb)
    top_i = jnp.transpose(ti, (0, 2, 1)).reshape(T, TOP_K)
    gates = jnp.transpose(tg, (0, 2, 1)).reshape(T, TOP_K)
    return xo, f, top_i, gates


def _conv_kernel(prev_ref, main_ref, next_ref, w_ref, b_ref, o_ref):
    i = pl.program_id(1)
    nl = pl.num_programs(1)
    tl = main_ref.shape[0]
    hr = prev_ref.shape[0]
    pv = jnp.where(i > 0, prev_ref[...].astype(F32), 0.0)
    nx = jnp.where(i < nl - 1, next_ref[...].astype(F32), 0.0)
    ext = jnp.concatenate([pv, main_ref[...].astype(F32), nx], axis=0)
    n = ext.shape[0]
    acc = None
    for k in range(SSD_CONV):
        off = k - SSD_CONV // 2
        sh = (-off) % n
        term = (ext if sh == 0 else pltpu.roll(ext, sh, 0))[hr:hr + tl] * w_ref[k:k + 1, :]
        acc = term if acc is None else acc + term
    acc = acc + b_ref[...]
    o_ref[...] = (acc * (1.0 / (1.0 + jnp.exp(-acc)))).astype(o_ref.dtype)


def _ssd_conv(p_main, conv_w, conv_b, Bsz, L, *, tl=512):
    T = p_main.shape[0]
    W = p_main.shape[1] // 4
    tl = min(tl, L)
    nl = L // tl
    hr = BF16_SUBLANES
    lh = L // hr
    th = tl // hr
    return pl.pallas_call(
        _conv_kernel, grid=(Bsz, nl, 2),
        in_specs=[
            pl.BlockSpec((hr, W), lambda b, i, c: (jnp.maximum(b * lh + i * th - 1, b * lh), 1 + c)),
            pl.BlockSpec((tl, W), lambda b, i, c: (b * nl + i, 1 + c)),
            pl.BlockSpec((hr, W), lambda b, i, c: (jnp.minimum(b * lh + (i + 1) * th, (b + 1) * lh - 1), 1 + c)),
            pl.BlockSpec((SSD_CONV, W), lambda b, i, c: (0, c)),
            pl.BlockSpec((1, W), lambda b, i, c: (0, c)),
        ],
        out_specs=pl.BlockSpec((tl, W), lambda b, i, c: (b * nl + i, c)),
        out_shape=jax.ShapeDtypeStruct((T, 2 * W), BF16),
        compiler_params=_params("parallel", "parallel", "parallel"), name="ssd_conv",
    )(p_main, p_main, p_main, conv_w, conv_b.reshape(1, -1))


def _split3(v):
    c1 = v.astype(BF16)
    r1 = v - c1.astype(F32)
    c2 = r1.astype(BF16)
    c3 = (r1 - c2.astype(F32)).astype(BF16)
    return jnp.concatenate([c1, c2, c3], axis=1)


def _scan_kernel(xbc_ref, dt_ref, tri_ref, alog_ref, dtb_ref, eh_ref, es_ref, h0_ref, y_ref, h_ref, *, width):
    c = pl.program_id(2)

    @pl.when(c == 0)
    def _():
        h_ref[...] = h0_ref[...]

    n_heads = width // SSD_HEAD_DIM
    hpg = n_heads // SSD_GROUPS
    gw = hpg * SSD_HEAD_DIM
    gn = SSD_GROUPS * SSD_STATE
    raw = dt_ref[0] + dtb_ref[0]
    dt = jnp.maximum(raw, 0.0) + jnp.log1p(jnp.exp(-jnp.abs(raw)))
    da = dt * (-jnp.exp(alog_ref[0]))
    tri = tri_ref[0]
    cum = jnp.dot(tri, da, precision=HIGHEST, preferred_element_type=F32)
    cum_t = lax.dot_general(da, tri, (((0,), (1,)), ((), ())),
                            precision=HIGHEST, preferred_element_type=F32)
    tot = jnp.broadcast_to(jnp.sum(da, axis=0, keepdims=True), (8, n_heads))
    cum3 = _split3(cum)
    cum_hd = jnp.dot(cum3, eh_ref[...], preferred_element_type=F32)
    dt_hd = jnp.dot(_split3(dt), eh_ref[...], preferred_element_type=F32)
    tot_hd = jnp.dot(_split3(tot), eh_ref[...], preferred_element_type=F32)[0:1]
    cum_st = jnp.dot(cum3, es_ref[...], preferred_element_type=F32)
    e_cum = jnp.exp(cum_hd)
    e_tot = jnp.exp(tot_hd)
    xdt = xbc_ref[:, :width].astype(F32) * dt_hd
    xdt_b = xdt.astype(BF16)
    xdt_end = (xdt * jnp.exp(tot_hd - cum_hd)).astype(BF16)
    mask = tri > 0.0
    first_head = lax.broadcasted_iota(jnp.int32, (SSD_CHUNK, 2 * SSD_HEAD_DIM), 1) < SSD_HEAD_DIM
    zero = jnp.zeros((), BF16)
    for g in range(SSD_GROUPS):
        bg = xbc_ref[:, width + g * SSD_STATE: width + (g + 1) * SSD_STATE]
        cg = xbc_ref[:, width + gn + g * SSD_STATE: width + gn + (g + 1) * SSD_STATE]
        gm = lax.dot_general(cg, bg, (((1,), (1,)), ((), ())), preferred_element_type=F32)
        h_in = h_ref[0, 0, g]
        y_off = jnp.dot(cg, h_in.astype(BF16), preferred_element_type=F32)
        for pair in range(hpg // 2):
            scores = []
            for hh in (2 * pair, 2 * pair + 1):
                h = g * hpg + hh
                seg = cum_st[:, h * SSD_CHUNK:(h + 1) * SSD_CHUNK] - cum_t[h:h + 1, :]
                scores.append((gm * jnp.exp(jnp.where(mask, seg, -jnp.inf))).astype(BF16))
            c0 = g * gw + pair * 2 * SSD_HEAD_DIM
            xp = xdt_b[:, c0:c0 + 2 * SSD_HEAD_DIM]
            rhs = jnp.concatenate([jnp.where(first_head, xp, zero), jnp.where(first_head, zero, xp)], axis=0)
            yp = jnp.dot(jnp.concatenate(scores, axis=1), rhs, preferred_element_type=F32)
            yp = yp + y_off[:, pair * 2 * SSD_HEAD_DIM:(pair + 1) * 2 * SSD_HEAD_DIM] * e_cum[:, c0:c0 + 2 * SSD_HEAD_DIM]
            y_ref[0, :, c0:c0 + 2 * SSD_HEAD_DIM] = yp.astype(y_ref.dtype)
        st = lax.dot_general(bg, xdt_end[:, g * gw:(g + 1) * gw], (((0,), (0,)), ((), ())),
                             preferred_element_type=F32)
        h_ref[0, 0, g] = h_in * e_tot[:, g * gw:(g + 1) * gw] + st


def _head_spread(n_heads, per_head):
    e = np.kron(np.eye(n_heads, dtype=np.float32), np.ones((1, per_head), np.float32))
    return jnp.asarray(np.concatenate([e, e, e], axis=0), dtype=BF16)


def _ssd_scan(xbc, dt2, tri, a_log, dt_bias, h0, Bsz, L, width):
    T = xbc.shape[0]
    H = width // SSD_HEAD_DIM
    nc = L // SSD_CHUNK

    def rows(d, b, c):
        return b * nc + c + d * (nc - 1 - 2 * c)

    st = pl.BlockSpec((1, 1) + h0.shape[2:], lambda d, b, c: (d, b, 0, 0, 0))
    vec = pl.BlockSpec((1, 1, H), lambda d, b, c: (d, 0, 0))
    return pl.pallas_call(
        functools.partial(_scan_kernel, width=width), grid=(2, Bsz, nc),
        in_specs=[
            pl.BlockSpec((SSD_CHUNK, 2 * width), lambda d, b, c: (rows(d, b, c), 0)),
            pl.BlockSpec((1, SSD_CHUNK, H), lambda d, b, c: (d, rows(d, b, c), 0)),
            pl.BlockSpec((1, SSD_CHUNK, SSD_CHUNK), lambda d, b, c: (d, 0, 0)),
            vec, vec,
            pl.BlockSpec((3 * H, width), lambda d, b, c: (0, 0)),
            pl.BlockSpec((3 * H, H * SSD_CHUNK), lambda d, b, c: (0, 0)),
            st,
        ],
        out_specs=[pl.BlockSpec((1, SSD_CHUNK, width), lambda d, b, c: (d, rows(d, b, c), 0)), st],
        out_shape=[jax.ShapeDtypeStruct((2, T, width), BF16),
                   jax.ShapeDtypeStruct(h0.shape, F32)],
        compiler_params=_params("parallel", "parallel", "arbitrary"), name="ssd_scan",
    )(xbc, dt2, tri, a_log.reshape(2, 1, H), dt_bias.reshape(2, 1, H),
      _head_spread(H, SSD_HEAD_DIM), _head_spread(H, SSD_CHUNK), h0)


def _ssd_out_kernel(y_ref, x_ref, z_ref, d_ref, nw_ref, o_ref):
    xs = x_ref[...].astype(F32)
    z = z_ref[...].astype(F32)
    y = y_ref[0].astype(F32) + y_ref[1].astype(F32) + xs * d_ref[...]
    y = y * (z * (1.0 / (1.0 + jnp.exp(-z))))
    gw = y.shape[1] // SSD_GROUPS
    for g in range(SSD_GROUPS):
        seg = y[:, g * gw:(g + 1) * gw]
        ms = jnp.mean(seg * seg, axis=-1, keepdims=True)
        o_ref[:, g * gw:(g + 1) * gw] = (seg * lax.rsqrt(ms + EPS) * nw_ref[:, g * gw:(g + 1) * gw]
                                         ).astype(o_ref.dtype)


def _ssd_out(y, xbc, p_main, d_skip, norm_w, *, tl=512):
    T = xbc.shape[0]
    W = xbc.shape[1] // 2
    tl = min(tl, T)
    row = lambda i: (i, 0)
    one = pl.BlockSpec((1, W), lambda i: (0, 0))
    return pl.pallas_call(
        _ssd_out_kernel, grid=(T // tl,),
        in_specs=[pl.BlockSpec((2, tl, W), lambda i: (0, i, 0)),
                  pl.BlockSpec((tl, W), row), pl.BlockSpec((tl, W), row), one, one],
        out_specs=pl.BlockSpec((tl, W), row),
        out_shape=jax.ShapeDtypeStruct((T, W), BF16),
        compiler_params=_params("parallel"), name="ssd_out",
    )(y, xbc, p_main, jnp.repeat(d_skip, SSD_HEAD_DIM).reshape(1, W), norm_w.reshape(1, W))


def _pool_kernel(prev_ref, v_ref, next_ref, band_ref, pw_ref, pb_ref, ps_ref, o_ref, *, rows):
    i = pl.program_id(1)
    nl = pl.num_programs(1)
    tl = v_ref.shape[0]
    gw = v_ref.shape[1] // len(POOL_WINDOWS)
    tile = band_ref.shape[1]
    pos = lax.broadcasted_iota(jnp.int32, (tl, 128), 0)
    r = i * (tl // GRID_W) + lax.shift_right_logical(pos, GRID_W.bit_length() - 1)
    cidx = lax.bitwise_and(pos, GRID_W - 1)
    for g, w in enumerate(POOL_WINDOWS):
        lo, hi = -(w // 2), w - 1 - w // 2
        cols = slice(g * gw, (g + 1) * gw)
        pv = jnp.where(i > 0, prev_ref[:, cols], jnp.zeros((), BF16))
        nx = jnp.where(i < nl - 1, next_ref[:, cols], jnp.zeros((), BF16))
        vg = v_ref[:, cols]
        ext = jnp.concatenate([pv, vg, nx], axis=0)
        band = band_ref[g]
        cs = jnp.concatenate(
            [jnp.dot(band, ext[t * tile:(t + 1) * tile], preferred_element_type=F32)
             for t in range(ext.shape[0] // tile)], axis=0)
        rs = None
        for o in range(lo, hi + 1):
            s0 = POOL_HALO + GRID_W * o
            term = cs[s0:s0 + tl]
            rs = term if rs is None else rs + term
        n_r = jnp.minimum(r + hi, rows - 1) - jnp.maximum(r + lo, 0) + 1
        n_c = jnp.minimum(cidx + hi, GRID_W - 1) - jnp.maximum(cidx + lo, 0) + 1
        inv = 1.0 / (n_r * n_c).astype(F32)
        inv = jnp.concatenate([inv] * (gw // 128), axis=1)
        d = rs * inv - vg.astype(F32)
        yv = jnp.dot(d.astype(BF16), pw_ref[g], preferred_element_type=F32) + pb_ref[:, cols]
        o_ref[:, cols] = (yv * ps_ref[:, cols]).astype(o_ref.dtype)


def _pool_bands():
    t = 2 * GRID_W
    i = np.arange(t)[:, None]
    j = np.arange(t)[None, :]
    out = []
    for w in POOL_WINDOWS:
        lo, hi = -(w // 2), w - 1 - w // 2
        dc = j % GRID_W - i % GRID_W
        out.append(((i // GRID_W == j // GRID_W) & (dc >= lo) & (dc <= hi)).astype(np.float32))
    return jnp.asarray(np.stack(out), dtype=BF16)


def _pool_mixer(p_main, pool_w, pool_b, pool_scale, Bsz, L, *, tl=1024):
    T = p_main.shape[0]
    W = p_main.shape[1] // 4
    tl = min(tl, L)
    assert L % tl == 0 and tl % POOL_HALO == 0
    nl = L // tl
    lh = L // POOL_HALO
    th = tl // POOL_HALO
    cb = 3
    full = lambda shape: pl.BlockSpec(shape, lambda b, i: (0,) * len(shape))
    return pl.pallas_call(
        functools.partial(_pool_kernel, rows=L // GRID_W), grid=(Bsz, nl),
        in_specs=[
            pl.BlockSpec((POOL_HALO, W), lambda b, i: (jnp.maximum(b * lh + i * th - 1, b * lh), cb)),
            pl.BlockSpec((tl, W), lambda b, i: (b * nl + i, cb)),
            pl.BlockSpec((POOL_HALO, W), lambda b, i: (jnp.minimum(b * lh + (i + 1) * th, (b + 1) * lh - 1), cb)),
            full((len(POOL_WINDOWS), 2 * GRID_W, 2 * GRID_W)),
            full(pool_w.shape), full((1, W)), full((1, W)),
        ],
        out_specs=pl.BlockSpec((tl, W), lambda b, i: (b * nl + i, 0)),
        out_shape=jax.ShapeDtypeStruct((T, W), BF16),
        compiler_params=_params("parallel", "parallel"), name="pool_mixer",
    )(p_main, p_main, p_main, _pool_bands(), pool_w.astype(BF16),
      pool_b.reshape(1, W), pool_scale.reshape(1, W))


def _gmlp_kernel(u_ref, v_ref, g_ref, b_ref, ws_ref, bs_ref, o_ref):
    v = v_ref[...].astype(F32)
    mu = jnp.mean(v, axis=-1, keepdims=True)
    vc = v - mu
    var = jnp.mean(vc * vc, axis=-1, keepdims=True)
    vn = (vc * lax.rsqrt(var + EPS) * g_ref[...] + b_ref[...]).astype(BF16)
    tl, W = v.shape
    hd = W // CG_HEADS
    for n in range(tl // CG_CHUNK):
        rs = slice(n * CG_CHUNK, (n + 1) * CG_CHUNK)
        for h in range(CG_HEADS):
            cs = slice(h * hd, (h + 1) * hd)
            sv = jnp.dot(ws_ref[h], vn[rs, cs], preferred_element_type=F32) + bs_ref[:, h:h + 1]
            o_ref[rs, cs] = (u_ref[rs, cs].astype(F32) * sv).astype(o_ref.dtype)


def _gmlp_gate(uv, ln_g, ln_b, ws, bs, *, tl=256):
    T = uv.shape[0]
    W = uv.shape[1] // 2
    tl = min(tl, T)
    one = pl.BlockSpec((1, W), lambda i: (0, 0))
    return pl.pallas_call(
        _gmlp_kernel, grid=(T // tl,),
        in_specs=[pl.BlockSpec((tl, W), lambda i: (i, 0)), pl.BlockSpec((tl, W), lambda i: (i, 1)),
                  one, one,
                  pl.BlockSpec(ws.shape, lambda i: (0, 0, 0)),
                  pl.BlockSpec((CG_CHUNK, CG_HEADS), lambda i: (0, 0))],
        out_specs=pl.BlockSpec((tl, W), lambda i: (i, 0)),
        out_shape=jax.ShapeDtypeStruct((T, W), BF16),
        compiler_params=_params("parallel"), name="gmlp_gate",
    )(uv, uv, ln_g.reshape(1, W), ln_b.reshape(1, W), ws.astype(BF16), jnp.transpose(bs))


def _gather_kernel(nu_ref, tok_ref, f_hbm, o_ref, buf, sem):
    tb = o_ref.shape[0]

    def row_copy(t):
        return pltpu.make_async_copy(f_hbm.at[pl.ds(tok_ref[0, 0, t], 1), :], buf.at[pl.ds(t, 1), :], sem)

    def issue(t, carry):
        row_copy(t).start()
        return carry

    def drain(t, carry):
        row_copy(t).wait()
        return carry

    @pl.when(pl.program_id(0) < nu_ref[0])
    def _():
        lax.fori_loop(0, tb, issue, 0, unroll=DMA_UNROLL)
        lax.fori_loop(0, tb, drain, 0, unroll=DMA_UNROLL)
        o_ref[...] = buf[...].astype(o_ref.dtype)


def _moe_gather(f, row_tok, n_used):
    T, D = f.shape
    P = row_tok.shape[0]
    nb = P // MOE_TM
    grid_spec = pltpu.PrefetchScalarGridSpec(
        num_scalar_prefetch=1, grid=(nb,),
        in_specs=[pl.BlockSpec((1, 1, MOE_TM), lambda i, nu: (i, 0, 0), memory_space=pltpu.SMEM),
                  pl.BlockSpec(memory_space=pl.ANY)],
        out_specs=pl.BlockSpec((MOE_TM, D), lambda i, nu: (i, 0)),
        scratch_shapes=[pltpu.VMEM((MOE_TM, D), F32), pltpu.SemaphoreType.DMA(())])
    return pl.pallas_call(
        _gather_kernel, grid_spec=grid_spec,
        out_shape=jax.ShapeDtypeStruct((P, D), BF16),
        compiler_params=_params("arbitrary"), name="moe_gather",
    )(n_used, row_tok.reshape(nb, 1, MOE_TM), f)


def _expert_changed(be_ref):
    i = pl.program_id(1)
    return jnp.logical_or(i == 0, be_ref[i] != be_ref[jnp.maximum(i - 1, 0)])


def _gu_kernel(be_ref, nu_ref, x_ref, wg_ref, wu_ref, bg_ref, bu_ref, o_ref, wg_bf, wu_bf):
    @pl.when(pl.program_id(1) < nu_ref[0])
    def _():
        @pl.when(_expert_changed(be_ref))
        def _():
            wg_bf[...] = wg_ref[0].astype(BF16)
            wu_bf[...] = wu_ref[0].astype(BF16)

        xv = x_ref[...]
        gv = jnp.dot(xv, wg_bf[...], preferred_element_type=F32) + bg_ref[0]
        uv = jnp.dot(xv, wu_bf[...], preferred_element_type=F32) + bu_ref[0]
        gv = jnp.minimum(gv, SWIGLU_LIMIT)
        uv = jnp.clip(uv, -SWIGLU_LIMIT, SWIGLU_LIMIT)
        act = gv * (1.0 / (1.0 + jnp.exp(-SWIGLU_ALPHA * gv))) * (uv + 1.0)
        o_ref[...] = act.astype(o_ref.dtype)


def _moe_gate_up(xs, block_e, n_used, w_gu, b_gu, *, tn=512):
    P, D = xs.shape
    F = w_gu.shape[2] // 2
    nj = F // tn
    nb = P // MOE_TM
    b3 = b_gu.reshape(-1, 1, 2 * F)
    grid_spec = pltpu.PrefetchScalarGridSpec(
        num_scalar_prefetch=2, grid=(nj, nb),
        in_specs=[pl.BlockSpec((MOE_TM, D), lambda j, i, be, nu: (i, 0)),
                  pl.BlockSpec((1, D, tn), lambda j, i, be, nu: (be[i], 0, j)),
                  pl.BlockSpec((1, D, tn), lambda j, i, be, nu: (be[i], 0, nj + j)),
                  pl.BlockSpec((1, 1, tn), lambda j, i, be, nu: (be[i], 0, j)),
                  pl.BlockSpec((1, 1, tn), lambda j, i, be, nu: (be[i], 0, nj + j))],
        out_specs=pl.BlockSpec((MOE_TM, tn), lambda j, i, be, nu: (i, j)),
        scratch_shapes=[pltpu.VMEM((D, tn), BF16), pltpu.VMEM((D, tn), BF16)])
    return pl.pallas_call(
        _gu_kernel, grid_spec=grid_spec,
        out_shape=jax.ShapeDtypeStruct((P, F), BF16),
        compiler_params=_params("arbitrary", "arbitrary"), name="moe_gate_up",
    )(block_e, n_used, xs, w_gu, w_gu, b3, b3)


def _down_kernel(be_ref, nu_ref, a_ref, w_ref, b_ref, o_ref, w_bf):
    @pl.when(pl.program_id(1) < nu_ref[0])
    def _():
        @pl.when(_expert_changed(be_ref))
        def _():
            w_bf[...] = w_ref[0].astype(BF16)

        o_ref[...] = jnp.dot(a_ref[...], w_bf[...], preferred_element_type=F32) + b_ref[0]


def _moe_down(act, block_e, n_used, w_down, b_down, *, tn=1024):
    P, F = act.shape
    D = w_down.shape[2]
    nb = P // MOE_TM
    grid_spec = pltpu.PrefetchScalarGridSpec(
        num_scalar_prefetch=2, grid=(D // tn, nb),
        in_specs=[pl.BlockSpec((MOE_TM, F), lambda j, i, be, nu: (i, 0)),
                  pl.BlockSpec((1, F, tn), lambda j, i, be, nu: (be[i], 0, j)),
                  pl.BlockSpec((1, 1, tn), lambda j, i, be, nu: (be[i], 0, j))],
        out_specs=pl.BlockSpec((MOE_TM, tn), lambda j, i, be, nu: (i, j)),
        scratch_shapes=[pltpu.VMEM((F, tn), BF16)])
    return pl.pallas_call(
        _down_kernel, grid_spec=grid_spec,
        out_shape=jax.ShapeDtypeStruct((P, D), F32),
        compiler_params=_params("arbitrary", "arbitrary"), name="moe_down",
    )(block_e, n_used, act, w_down, b_down.reshape(-1, 1, D))


def _combine_kernel(dest_ref, gt_ref, x_ref, g_ref, sc_ref, sh_ref, ys_hbm, *rest, want_x):
    if want_x:
        xo_ref, h_ref, buf, sem = rest
    else:
        h_ref, buf, sem = rest
    tb = x_ref.shape[0]

    def row_copy(t, k):
        return pltpu.make_async_copy(ys_hbm.at[pl.ds(dest_ref[0, 0, t * TOP_K + k], 1), :],
                                     buf.at[k, pl.ds(t, 1), :], sem)

    def issue(t, carry):
        for k in range(TOP_K):
            row_copy(t, k).start()
        return carry

    def drain(t, carry):
        for k in range(TOP_K):
            row_copy(t, k).wait()
        return carry

    lax.fori_loop(0, tb, issue, 0, unroll=DMA_UNROLL // TOP_K)
    lax.fori_loop(0, tb, drain, 0, unroll=DMA_UNROLL // TOP_K)
    moe = None
    for k in range(TOP_K):
        term = buf[k] * gt_ref[:, k:k + 1]
        moe = term if moe is None else moe + term
    xv = x_ref[...] + g_ref[0] * moe
    if want_x:
        xo_ref[...] = xv
    h_ref[...] = _rms_mod(xv, sc_ref[0], sh_ref[0]).astype(h_ref.dtype)


def _moe_combine(ys, dest, gates, x2, g, sc, sh, L, *, want_x, out_dtype, tb=256):
    T, D = x2.shape
    tb = min(tb, L)
    nl = L // tb
    mod = pl.BlockSpec((1, 1, D), lambda i: (i // nl, 0, 0))
    row = pl.BlockSpec((tb, D), lambda i: (i, 0))
    out_specs = [row, row] if want_x else [row]
    out_shape = ([jax.ShapeDtypeStruct((T, D), F32)] if want_x else []) + [jax.ShapeDtypeStruct((T, D), out_dtype)]
    return pl.pallas_call(
        functools.partial(_combine_kernel, want_x=want_x), grid=(T // tb,),
        in_specs=[pl.BlockSpec((1, 1, tb * TOP_K), lambda i: (i, 0, 0), memory_space=pltpu.SMEM),
                  pl.BlockSpec((tb, TOP_K), lambda i: (i, 0)),
                  row, mod, mod, mod, pl.BlockSpec(memory_space=pl.ANY)],
        out_specs=out_specs, out_shape=out_shape,
        scratch_shapes=[pltpu.VMEM((TOP_K, tb, D), F32), pltpu.SemaphoreType.DMA(())],
        compiler_params=_params("arbitrary"), name="moe_combine",
    )(dest.reshape(T // tb, 1, tb * TOP_K), gates, x2, g, sc, sh, ys)


def _moe_routing(top_i):
    T = top_i.shape[0]
    A = T * TOP_K
    P = A + N_EXPERTS * MOE_TM
    nb = P // MOE_TM
    flat_e = top_i.reshape(A)
    onehot = (flat_e[:, None] == jnp.arange(N_EXPERTS, dtype=jnp.int32)[None, :]).astype(jnp.int32)
    cs = jnp.cumsum(onehot, axis=0)
    rank = jnp.take_along_axis(cs, flat_e[:, None], axis=1)[:, 0] - 1
    counts = cs[-1]
    padded = (counts + MOE_TM - 1) // MOE_TM * MOE_TM
    pad_end = jnp.cumsum(padded)
    pad_start = pad_end - padded
    dest = (pad_start[flat_e] + rank).astype(jnp.int32)
    block_e = jnp.minimum(jnp.searchsorted(pad_end, jnp.arange(nb, dtype=jnp.int32) * MOE_TM, side="right"),
                          N_EXPERTS - 1).astype(jnp.int32)
    n_used = (pad_end[-1:] // MOE_TM).astype(jnp.int32)
    _, sorted_tok = lax.sort((dest, jnp.arange(A, dtype=jnp.int32) // TOP_K), num_keys=1)
    start = jnp.cumsum(counts) - counts
    row = jnp.arange(P, dtype=jnp.int32)
    row_e = jnp.repeat(block_e, MOE_TM)
    off = row - pad_start[row_e]
    src = jnp.clip(start[row_e] + off, 0, A - 1)
    row_tok = jnp.where(off < counts[row_e], sorted_tok[src], 0).astype(jnp.int32)
    return dest, row_tok, block_e, n_used


def _moe(x2, m, g1, sc2, sh2, g2, nsc, nsh, router_w, router_b, w_gu, b_gu, w_down, b_down, L,
         *, layer, want_x, out_dtype):
    xo, f, top_i, gates = _res_norm_router(x2, m, g1, sc2, sh2, jnp.transpose(router_w),
                                           router_b.reshape(N_EXPERTS, 1), L)
    dest, row_tok, block_e, n_used = _moe_routing(top_i)
    xs = _moe_gather(f, row_tok, n_used)
    block_w = block_e + layer * N_EXPERTS
    act = _moe_gate_up(xs, block_w, n_used, w_gu.reshape((-1,) + w_gu.shape[2:]), b_gu)
    ys = _moe_down(act, block_w, n_used, w_down.reshape((-1,) + w_down.shape[2:]), b_down)
    return _moe_combine(ys, dest, gates, xo, g2, nsc, nsh, L, want_x=want_x, out_dtype=out_dtype)


def _ssd_tri():
    l = np.arange(SSD_CHUNK)[:, None]
    j = np.arange(SSD_CHUNK)[None, :]
    return jnp.asarray(np.stack([(j <= l), (j >= l)]).astype(np.float32))


def _ssd_stream(h_bf, w_main, w_dt, conv_w, conv_b, a_log, dt_bias, h0, Bsz, L, width):
    T = h_bf.shape[0]
    H = width // SSD_HEAD_DIM
    p_main = _matmul([(h_bf, w_main)], tm=1024, tn=1024, out_dtype=BF16, name="even_in_proj")
    dt_raw = _matmul([(h_bf, w_dt)], tm=512, tn=128, out_dtype=F32, precision=HIGHEST, name="dt_proj")
    dt2 = jnp.transpose(dt_raw[:, :2 * H].reshape(T, 2, H), (1, 0, 2))
    xbc = _ssd_conv(p_main, conv_w, conv_b, Bsz, L)
    y, h_last = _ssd_scan(xbc, dt2, _ssd_tri(), a_log, dt_bias, h0, Bsz, L, width)
    return p_main, xbc, y, h_last


def kernel(x, c, ctx, c_ctx, ada_w, ada_b, even_in_w, ssd_conv_w, ssd_conv_b, ssd_a_log, ssd_dt_bias,
           ssd_d, ssd_norm_w, pool_w, pool_b, pool_scale, even_out_w, cg_in_w, cg_ln_g, cg_ln_b, cg_ws,
           cg_bs, cg_out_w, router_w, router_b, moe_w_gu, moe_b_gu, moe_w_down, moe_b_down, final_norm_w):
    Bsz, S, D = x.shape
    Lc = ctx.shape[1]
    T = Bsz * S
    width = D
    H = width // SSD_HEAD_DIM
    gn = SSD_GROUPS * SSD_STATE
    x2 = x.reshape(T, D)
    ctx2 = ctx.reshape(Bsz * Lc, D)

    cin = jnp.zeros((8, D), F32).at[:Bsz].set(c).at[Bsz].set(c_ctx)
    mods = [_matmul([(cin, ada_w[i])], tm=8, tn=1024, out_dtype=F32, bias=ada_b[i], pre_act="silu",
                    precision=HIGHEST, name="ada_mod") for i in range(DEPTH)]

    def lat_mods(i):
        return [mods[i][:Bsz, k * D:(k + 1) * D].reshape(Bsz, 1, D) for k in range(6)]

    sh1, sc1, g1, sh2, sc2, g2 = lat_mods(0)
    sh_c = jnp.broadcast_to(mods[0][Bsz, :D].reshape(1, 1, D), (Bsz, 1, D))
    sc_c = jnp.broadcast_to(mods[0][Bsz, D:2 * D].reshape(1, 1, D), (Bsz, 1, D))
    w_in = even_in_w[0]
    o_dt = 2 * width + 2 * gn
    w_main = jnp.concatenate([w_in[:, :o_dt], w_in[:, o_dt + 2 * H:]], axis=1).astype(BF16)
    w_dt = jnp.pad(w_in[:, o_dt:o_dt + 2 * H], ((0, 0), (0, 128 - 2 * H)))
    ssd_p = (w_main, w_dt, ssd_conv_w[0], ssd_conv_b[0], ssd_a_log[0], ssd_dt_bias[0])

    h_c = _norm_mod(ctx2, sc_c, sh_c, Lc)
    h0 = jnp.zeros((2, Bsz, SSD_GROUPS, SSD_STATE, width // SSD_GROUPS), F32)
    _, _, _, ctx_state = _ssd_stream(h_c, *ssd_p, h0, Bsz, Lc, width)

    h_l = _norm_mod(x2, sc1, sh1, S)
    p_main, xbc, y, _ = _ssd_stream(h_l, *ssd_p, ctx_state, Bsz, S, width)
    y_ssd = _ssd_out(y, xbc, p_main, ssd_d[0], ssd_norm_w[0])
    y_pool = _pool_mixer(p_main, pool_w[0], pool_b[0], pool_scale[0], Bsz, S)
    w_out = even_out_w[0].astype(BF16)
    m = _matmul([(y_ssd, w_out[:width]), (y_pool, w_out[width:])], tm=1024, tn=1024, out_dtype=F32,
                name="even_out_proj")

    nsh1, nsc1, ng1, nsh2, nsc2, ng2 = lat_mods(1)
    x2, h_l = _moe(x2, m, g1, sc2, sh2, g2, nsc1, nsh1, router_w[0], router_b[0], moe_w_gu, moe_b_gu,
                   moe_w_down, moe_b_down, S, layer=0, want_x=True, out_dtype=BF16)

    uv = _matmul([(h_l, cg_in_w[0].astype(BF16))], tm=1024, tn=1024, out_dtype=BF16, act="gelu",
                 name="cg_in_proj")
    gated = _gmlp_gate(uv, cg_ln_g[0], cg_ln_b[0], cg_ws[0], cg_bs[0])
    m = _matmul([(gated, cg_out_w[0].astype(BF16))], tm=1024, tn=1024, out_dtype=F32, name="cg_out_proj")
    fin_sc = jnp.broadcast_to((final_norm_w - 1.0).reshape(1, 1, D), (Bsz, 1, D))
    fin_sh = jnp.zeros((Bsz, 1, D), F32)
    (out,) = _moe(x2, m, ng1, nsc2, nsh2, ng2, fin_sc, fin_sh, router_w[1], router_b[1], moe_w_gu,
                  moe_b_gu, moe_w_down, moe_b_down, S, layer=1, want_x=False, out_dtype=F32)
    return out.reshape(Bsz, S, D)
```

```python
import functools
import math

import numpy as np
import jax
import jax.numpy as jnp
from jax import lax
from jax.experimental import pallas as pl
from jax.experimental.pallas import tpu as pltpu

F32 = jnp.float32
BF16 = jnp.bfloat16
HIGHEST = lax.Precision.HIGHEST

GRID_W = 64
EPS = 1e-6
DEPTH = 2

SSD_HEAD_DIM = 64
SSD_GROUPS = 8
SSD_STATE = 128
SSD_CONV = 4
SSD_CHUNK = 128

POOL_WINDOWS = (2, 4, 8, 16)
POOL_HALO = 512

CG_HEADS = 8
CG_CHUNK = 128

N_EXPERTS = 32
TOP_K = 4
SWIGLU_ALPHA = 1.702
SWIGLU_LIMIT = 7.0
MOE_TM = 512
DMA_UNROLL = 8

V7X_VMEM_LIMIT_BYTES = 56 * 1024 * 1024
BF16_SUBLANES = 16


def _params(*sem):
    return pltpu.CompilerParams(dimension_semantics=sem, vmem_limit_bytes=V7X_VMEM_LIMIT_BYTES)


def _mm_kernel(*refs, n_pairs, has_bias, pre_act, act, precision, fold):
    o_ref = refs[-1]
    acc = None
    for p in range(n_pairs):
        xv = refs[2 * p][...]
        wv = refs[2 * p + 1][...]
        if precision is not None:
            xv = xv.astype(F32)
        if pre_act == "silu":
            xv = xv * (1.0 / (1.0 + jnp.exp(-xv)))
        d = jnp.dot(xv, wv, preferred_element_type=F32, precision=precision)
        acc = d if acc is None else acc + d
    if has_bias:
        acc = acc + refs[2 * n_pairs][...]
    if act == "gelu":
        acc = 0.5 * acc * (1.0 + lax.erf(acc * (1.0 / math.sqrt(2.0))))
    if fold > 1:
        wn = acc.shape[1] // fold
        acc = sum(acc[:, k * wn:(k + 1) * wn] for k in range(fold))
    o_ref[...] = acc.astype(o_ref.dtype)


def _matmul(pairs, *, tm, tn, out_dtype, name, bias=None, pre_act=None, act=None, precision=None, fold=1):
    M = pairs[0][0].shape[0]
    N = pairs[0][1].shape[1]
    tm = min(tm, M)
    tn = min(tn, N)
    assert M % tm == 0 and N % tn == 0 and (fold == 1 or tn == N)
    in_specs, args = [], []
    for xa, wa in pairs:
        K = xa.shape[1]
        in_specs += [pl.BlockSpec((tm, K), lambda i, j: (i, 0)),
                     pl.BlockSpec((K, tn), lambda i, j: (0, j))]
        args += [xa, wa]
    if bias is not None:
        in_specs.append(pl.BlockSpec((1, tn), lambda i, j: (0, j)))
        args.append(bias.reshape(1, N))
    kern = functools.partial(_mm_kernel, n_pairs=len(pairs), has_bias=bias is not None,
                             pre_act=pre_act, act=act, precision=precision, fold=fold)
    return pl.pallas_call(
        kern, grid=(M // tm, N // tn), in_specs=in_specs,
        out_specs=pl.BlockSpec((tm, tn // fold), lambda i, j: (i, j)),
        out_shape=jax.ShapeDtypeStruct((M, N // fold), out_dtype),
        compiler_params=_params("parallel", "parallel"), name=name)(*args)


def _rms_mod(xv, sc, sh):
    ms = jnp.mean(xv * xv, axis=-1, keepdims=True)
    return xv * lax.rsqrt(ms + EPS) * (1.0 + sc) + sh


def _norm_kernel(x_ref, sc_ref, sh_ref, h_ref):
    h_ref[...] = _rms_mod(x_ref[...], sc_ref[0], sh_ref[0]).astype(h_ref.dtype)


def _norm_mod(x2, sc, sh, L, *, tl=512):
    T, D = x2.shape
    tl = min(tl, L)
    nl = L // tl
    mod = pl.BlockSpec((1, 1, D), lambda i: (i // nl, 0, 0))
    return pl.pallas_call(
        _norm_kernel, grid=(T // tl,),
        in_specs=[pl.BlockSpec((tl, D), lambda i: (i, 0)), mod, mod],
        out_specs=pl.BlockSpec((tl, D), lambda i: (i, 0)),
        out_shape=jax.ShapeDtypeStruct((T, D), BF16),
        compiler_params=_params("parallel"), name="norm_mod")(x2, sc, sh)


def _res_router_kernel(x_ref, m_ref, g_ref, sc_ref, sh_ref, rw_ref, rb_ref,
                       xo_ref, f_ref, ti_ref, tg_ref):
    xv = x_ref[...] + g_ref[0] * m_ref[...]
    xo_ref[...] = xv
    f = _rms_mod(xv, sc_ref[0], sh_ref[0])
    f_ref[...] = f
    lg = lax.dot_general(rw_ref[...], f, (((1,), (1,)), ((), ())),
                         precision=HIGHEST, preferred_element_type=F32) + rb_ref[...]
    iota = lax.broadcasted_iota(jnp.int32, lg.shape, 0)
    vals, idxs = [], []
    for _ in range(TOP_K):
        mx = jnp.max(lg, axis=0, keepdims=True)
        ix = jnp.min(jnp.where(lg == mx, iota, N_EXPERTS), axis=0, keepdims=True)
        vals.append(mx)
        idxs.append(ix)
        lg = jnp.where(iota == ix, -jnp.inf, lg)
    tv = jnp.concatenate(vals, axis=0)
    ex = jnp.exp(tv - vals[0])
    ti_ref[0] = jnp.concatenate(idxs, axis=0)
    tg_ref[0] = ex / jnp.sum(ex, axis=0, keepdims=True)


def _res_norm_router(x2, m, g, sc, sh, rw_t, rb, L, *, tl=512):
    T, D = x2.shape
    tl = min(tl, L)
    nl = L // tl
    nb = T // tl
    mod = pl.BlockSpec((1, 1, D), lambda i: (i // nl, 0, 0))
    row = pl.BlockSpec((tl, D), lambda i: (i, 0))
    tk = pl.BlockSpec((1, TOP_K, tl), lambda i: (i, 0, 0))
    xo, f, ti, tg = pl.pallas_call(
        _res_router_kernel, grid=(nb,),
        in_specs=[row, row, mod, mod, mod,
                  pl.BlockSpec((N_EXPERTS, D), lambda i: (0, 0)),
                  pl.BlockSpec((N_EXPERTS, 1), lambda i: (0, 0))],
        out_specs=[row, row, tk, tk],
        out_shape=[jax.ShapeDtypeStruct((T, D), F32), jax.ShapeDtypeStruct((T, D), F32),
                   jax.ShapeDtypeStruct((nb, TOP_K, tl), jnp.int32),
                   jax.ShapeDtypeStruct((nb, TOP_K, tl), F32)],
        compiler_params=_params("parallel"), name="res_norm_router")(x2, m, g, sc, sh, rw_t, rb)
    top_i = jnp.transpose(ti, (0, 2, 1)).reshape(T, TOP_K)
    gates = jnp.transpose(tg, (0, 2, 1)).reshape(T, TOP_K)
    return xo, f, top_i, gates


def _conv_kernel(prev_ref, main_ref, next_ref, w_ref, b_ref, o_ref):
    i = pl.program_id(1)
    nl = pl.num_programs(1)
    tl = main_ref.shape[0]
    hr = prev_ref.shape[0]
    pv = jnp.where(i > 0, prev_ref[...].astype(F32), 0.0)
    nx = jnp.where(i < nl - 1, next_ref[...].astype(F32), 0.0)
    ext = jnp.concatenate([pv, main_ref[...].astype(F32), nx], axis=0)
    n = ext.shape[0]
    acc = None
    for k in range(SSD_CONV):
        off = k - SSD_CONV // 2
        sh = (-off) % n
        term = (ext if sh == 0 else pltpu.roll(ext, sh, 0))[hr:hr + tl] * w_ref[k:k + 1, :]
        acc = term if acc is None else acc + term
    acc = acc + b_ref[...]
    o_ref[...] = (acc * (1.0 / (1.0 + jnp.exp(-acc)))).astype(o_ref.dtype)


def _ssd_conv(p_main, conv_w, conv_b, Bsz, L, *, tl=512):
    T = p_main.shape[0]
    W = p_main.shape[1] // 4
    tl = min(tl, L)
    nl = L // tl
    hr = BF16_SUBLANES
    lh = L // hr
    th = tl // hr
    return pl.pallas_call(
        _conv_kernel, grid=(Bsz, nl, 2),
        in_specs=[
            pl.BlockSpec((hr, W), lambda b, i, c: (jnp.maximum(b * lh + i * th - 1, b * lh), 1 + c)),
            pl.BlockSpec((tl, W), lambda b, i, c: (b * nl + i, 1 + c)),
            pl.BlockSpec((hr, W), lambda b, i, c: (jnp.minimum(b * lh + (i + 1) * th, (b + 1) * lh - 1), 1 + c)),
            pl.BlockSpec((SSD_CONV, W), lambda b, i, c: (0, c)),
            pl.BlockSpec((1, W), lambda b, i, c: (0, c)),
        ],
        out_specs=pl.BlockSpec((tl, W), lambda b, i, c: (b * nl + i, c)),
        out_shape=jax.ShapeDtypeStruct((T, 2 * W), BF16),
        compiler_params=_params("parallel", "parallel", "parallel"), name="ssd_conv",
    )(p_main, p_main, p_main, conv_w, conv_b.reshape(1, -1))


def _split3(v):
    c1 = v.astype(BF16)
    r1 = v - c1.astype(F32)
    c2 = r1.astype(BF16)
    c3 = (r1 - c2.astype(F32)).astype(BF16)
    return jnp.concatenate([c1, c2, c3], axis=1)


def _scan_kernel(xbc_ref, dt_ref, tri_ref, alog_ref, dtb_ref, eh_ref, es_ref, h0_ref, y_ref, h_ref, *, width):
    c = pl.program_id(2)

    @pl.when(c == 0)
    def _():
        h_ref[...] = h0_ref[...]

    n_heads = width // SSD_HEAD_DIM
    hpg = n_heads // SSD_GROUPS
    gw = hpg * SSD_HEAD_DIM
    gn = SSD_GROUPS * SSD_STATE
    raw = dt_ref[0] + dtb_ref[0]
    dt = jnp.maximum(raw, 0.0) + jnp.log1p(jnp.exp(-jnp.abs(raw)))
    da = dt * (-jnp.exp(alog_ref[0]))
    tri = tri_ref[0]
    cum = jnp.dot(tri, da, precision=HIGHEST, preferred_element_type=F32)
    cum_t = lax.dot_general(da, tri, (((0,), (1,)), ((), ())),
                            precision=HIGHEST, preferred_element_type=F32)
    tot = jnp.broadcast_to(jnp.sum(da, axis=0, keepdims=True), (8, n_heads))
    cum3 = _split3(cum)
    cum_hd = jnp.dot(cum3, eh_ref[...], preferred_element_type=F32)
    dt_hd = jnp.dot(_split3(dt), eh_ref[...], preferred_element_type=F32)
    tot_hd = jnp.dot(_split3(tot), eh_ref[...], preferred_element_type=F32)[0:1]
    cum_st = jnp.dot(cum3, es_ref[...], preferred_element_type=F32)
    e_cum = jnp.exp(cum_hd)
    e_tot = jnp.exp(tot_hd)
    xdt = xbc_ref[:, :width].astype(F32) * dt_hd
    xdt_b = xdt.astype(BF16)
    xdt_end = (xdt * jnp.exp(tot_hd - cum_hd)).astype(BF16)
    mask = tri > 0.0
    first_head = lax.broadcasted_iota(jnp.int32, (SSD_CHUNK, 2 * SSD_HEAD_DIM), 1) < SSD_HEAD_DIM
    zero = jnp.zeros((), BF16)
    for g in range(SSD_GROUPS):
        bg = xbc_ref[:, width + g * SSD_STATE: width + (g + 1) * SSD_STATE]
        cg = xbc_ref[:, width + gn + g * SSD_STATE: width + gn + (g + 1) * SSD_STATE]
        gm = lax.dot_general(cg, bg, (((1,), (1,)), ((), ())), preferred_element_type=F32)
        h_in = h_ref[0, 0, g]
        y_off = jnp.dot(cg, h_in.astype(BF16), preferred_element_type=F32)
        for pair in range(hpg // 2):
            scores = []
            for hh in (2 * pair, 2 * pair + 1):
                h = g * hpg + hh
                seg = cum_st[:, h * SSD_CHUNK:(h + 1) * SSD_CHUNK] - cum_t[h:h + 1, :]
                scores.append((gm * jnp.exp(jnp.where(mask, seg, -jnp.inf))).astype(BF16))
            c0 = g * gw + pair * 2 * SSD_HEAD_DIM
            xp = xdt_b[:, c0:c0 + 2 * SSD_HEAD_DIM]
            rhs = jnp.concatenate([jnp.where(first_head, xp, zero), jnp.where(first_head, zero, xp)], axis=0)
            yp = jnp.dot(jnp.concatenate(scores, axis=1), rhs, preferred_element_type=F32)
            yp = yp + y_off[:, pair * 2 * SSD_HEAD_DIM:(pair + 1) * 2 * SSD_HEAD_DIM] * e_cum[:, c0:c0 + 2 * SSD_HEAD_DIM]
            y_ref[0, :, c0:c0 + 2 * SSD_HEAD_DIM] = yp.astype(y_ref.dtype)
        st = lax.dot_general(bg, xdt_end[:, g * gw:(g + 1) * gw], (((0,), (0,)), ((), ())),
                             preferred_element_type=F32)
        h_ref[0, 0, g] = h_in * e_tot[:, g * gw:(g + 1) * gw] + st


def _head_spread(n_heads, per_head):
    e = np.kron(np.eye(n_heads, dtype=np.float32), np.ones((1, per_head), np.float32))
    return jnp.asarray(np.concatenate([e, e, e], axis=0), dtype=BF16)


def _ssd_scan(xbc, dt2, tri, a_log, dt_bias, h0, Bsz, L, width):
    T = xbc.shape[0]
    H = width // SSD_HEAD_DIM
    nc = L // SSD_CHUNK

    def rows(d, b, c):
        return b * nc + c + d * (nc - 1 - 2 * c)

    st = pl.BlockSpec((1, 1) + h0.shape[2:], lambda d, b, c: (d, b, 0, 0, 0))
    vec = pl.BlockSpec((1, 1, H), lambda d, b, c: (d, 0, 0))
    return pl.pallas_call(
        functools.partial(_scan_kernel, width=width), grid=(2, Bsz, nc),
        in_specs=[
            pl.BlockSpec((SSD_CHUNK, 2 * width), lambda d, b, c: (rows(d, b, c), 0)),
            pl.BlockSpec((1, SSD_CHUNK, H), lambda d, b, c: (d, rows(d, b, c), 0)),
            pl.BlockSpec((1, SSD_CHUNK, SSD_CHUNK), lambda d, b, c: (d, 0, 0)),
            vec, vec,
            pl.BlockSpec((3 * H, width), lambda d, b, c: (0, 0)),
            pl.BlockSpec((3 * H, H * SSD_CHUNK), lambda d, b, c: (0, 0)),
            st,
        ],
        out_specs=[pl.BlockSpec((1, SSD_CHUNK, width), lambda d, b, c: (d, rows(d, b, c), 0)), st],
        out_shape=[jax.ShapeDtypeStruct((2, T, width), BF16),
                   jax.ShapeDtypeStruct(h0.shape, F32)],
        compiler_params=_params("parallel", "parallel", "arbitrary"), name="ssd_scan",
    )(xbc, dt2, tri, a_log.reshape(2, 1, H), dt_bias.reshape(2, 1, H),
      _head_spread(H, SSD_HEAD_DIM), _head_spread(H, SSD_CHUNK), h0)


def _ssd_out_kernel(y_ref, x_ref, z_ref, d_ref, nw_ref, o_ref):
    xs = x_ref[...].astype(F32)
    z = z_ref[...].astype(F32)
    y = y_ref[0].astype(F32) + y_ref[1].astype(F32) + xs * d_ref[...]
    y = y * (z * (1.0 / (1.0 + jnp.exp(-z))))
    gw = y.shape[1] // SSD_GROUPS
    for g in range(SSD_GROUPS):
        seg = y[:, g * gw:(g + 1) * gw]
        ms = jnp.mean(seg * seg, axis=-1, keepdims=True)
        o_ref[:, g * gw:(g + 1) * gw] = (seg * lax.rsqrt(ms + EPS) * nw_ref[:, g * gw:(g + 1) * gw]
                                         ).astype(o_ref.dtype)


def _ssd_out(y, xbc, p_main, d_skip, norm_w, *, tl=512):
    T = xbc.shape[0]
    W = xbc.shape[1] // 2
    tl = min(tl, T)
    row = lambda i: (i, 0)
    one = pl.BlockSpec((1, W), lambda i: (0, 0))
    return pl.pallas_call(
        _ssd_out_kernel, grid=(T // tl,),
        in_specs=[pl.BlockSpec((2, tl, W), lambda i: (0, i, 0)),
                  pl.BlockSpec((tl, W), row), pl.BlockSpec((tl, W), row), one, one],
        out_specs=pl.BlockSpec((tl, W), row),
        out_shape=jax.ShapeDtypeStruct((T, W), BF16),
        compiler_params=_params("parallel"), name="ssd_out",
    )(y, xbc, p_main, jnp.repeat(d_skip, SSD_HEAD_DIM).reshape(1, W), norm_w.reshape(1, W))


def _pool_kernel(prev_ref, v_ref, next_ref, band_ref, pw_ref, pb_ref, ps_ref, o_ref, *, rows):
    i = pl.program_id(1)
    nl = pl.num_programs(1)
    tl = v_ref.shape[0]
    gw = v_ref.shape[1] // len(POOL_WINDOWS)
    tile = band_ref.shape[1]
    pos = lax.broadcasted_iota(jnp.int32, (tl, 128), 0)
    r = i * (tl // GRID_W) + lax.shift_right_logical(pos, GRID_W.bit_length() - 1)
    cidx = lax.bitwise_and(pos, GRID_W - 1)
    for g, w in enumerate(POOL_WINDOWS):
        lo, hi = -(w // 2), w - 1 - w // 2
        cols = slice(g * gw, (g + 1) * gw)
        pv = jnp.where(i > 0, prev_ref[:, cols], jnp.zeros((), BF16))
        nx = jnp.where(i < nl - 1, next_ref[:, cols], jnp.zeros((), BF16))
        vg = v_ref[:, cols]
        ext = jnp.concatenate([pv, vg, nx], axis=0)
        band = band_ref[g]
        cs = jnp.concatenate(
            [jnp.dot(band, ext[t * tile:(t + 1) * tile], preferred_element_type=F32)
             for t in range(ext.shape[0] // tile)], axis=0)
        rs = None
        for o in range(lo, hi + 1):
            s0 = POOL_HALO + GRID_W * o
            term = cs[s0:s0 + tl]
            rs = term if rs is None else rs + term
        n_r = jnp.minimum(r + hi, rows - 1) - jnp.maximum(r + lo, 0) + 1
        n_c = jnp.minimum(cidx + hi, GRID_W - 1) - jnp.maximum(cidx + lo, 0) + 1
        inv = 1.0 / (n_r * n_c).astype(F32)
        inv = jnp.concatenate([inv] * (gw // 128), axis=1)
        d = rs * inv - vg.astype(F32)
        yv = jnp.dot(d.astype(BF16), pw_ref[g], preferred_element_type=F32) + pb_ref[:, cols]
        o_ref[:, cols] = (yv * ps_ref[:, cols]).astype(o_ref.dtype)


def _pool_bands():
    t = 2 * GRID_W
    i = np.arange(t)[:, None]
    j = np.arange(t)[None, :]
    out = []
    for w in POOL_WINDOWS:
        lo, hi = -(w // 2), w - 1 - w // 2
        dc = j % GRID_W - i % GRID_W
        out.append(((i // GRID_W == j // GRID_W) & (dc >= lo) & (dc <= hi)).astype(np.float32))
    return jnp.asarray(np.stack(out), dtype=BF16)


def _pool_mixer(p_main, pool_w, pool_b, pool_scale, Bsz, L, *, tl=1024):
    T = p_main.shape[0]
    W = p_main.shape[1] // 4
    tl = min(tl, L)
    assert L % tl == 0 and tl % POOL_HALO == 0
    nl = L // tl
    lh = L // POOL_HALO
    th = tl // POOL_HALO
    cb = 3
    full = lambda shape: pl.BlockSpec(shape, lambda b, i: (0,) * len(shape))
    return pl.pallas_call(
        functools.partial(_pool_kernel, rows=L // GRID_W), grid=(Bsz, nl),
        in_specs=[
            pl.BlockSpec((POOL_HALO, W), lambda b, i: (jnp.maximum(b * lh + i * th - 1, b * lh), cb)),
            pl.BlockSpec((tl, W), lambda b, i: (b * nl + i, cb)),
            pl.BlockSpec((POOL_HALO, W), lambda b, i: (jnp.minimum(b * lh + (i + 1) * th, (b + 1) * lh - 1), cb)),
            full((len(POOL_WINDOWS), 2 * GRID_W, 2 * GRID_W)),
            full(pool_w.shape), full((1, W)), full((1, W)),
        ],
        out_specs=pl.BlockSpec((tl, W), lambda b, i: (b * nl + i, 0)),
        out_shape=jax.ShapeDtypeStruct((T, W), BF16),
        compiler_params=_params("parallel", "parallel"), name="pool_mixer",
    )(p_main, p_main, p_main, _pool_bands(), pool_w.astype(BF16),
      pool_b.reshape(1, W), pool_scale.reshape(1, W))


def _gmlp_kernel(u_ref, v_ref, g_ref, b_ref, ws_ref, bs_ref, o_ref):
    v = v_ref[...].astype(F32)
    mu = jnp.mean(v, axis=-1, keepdims=True)
    vc = v - mu
    var = jnp.mean(vc * vc, axis=-1, keepdims=True)
    vn = (vc * lax.rsqrt(var + EPS) * g_ref[...] + b_ref[...]).astype(BF16)
    tl, W = v.shape
    hd = W // CG_HEADS
    for n in range(tl // CG_CHUNK):
        rs = slice(n * CG_CHUNK, (n + 1) * CG_CHUNK)
        for h in range(CG_HEADS):
            cs = slice(h * hd, (h + 1) * hd)
            sv = jnp.dot(ws_ref[h], vn[rs, cs], preferred_element_type=F32) + bs_ref[:, h:h + 1]
            o_ref[rs, cs] = (u_ref[rs, cs].astype(F32) * sv).astype(o_ref.dtype)


def _gmlp_gate(uv, ln_g, ln_b, ws, bs, *, tl=256):
    T = uv.shape[0]
    W = uv.shape[1] // 2
    tl = min(tl, T)
    one = pl.BlockSpec((1, W), lambda i: (0, 0))
    return pl.pallas_call(
        _gmlp_kernel, grid=(T // tl,),
        in_specs=[pl.BlockSpec((tl, W), lambda i: (i, 0)), pl.BlockSpec((tl, W), lambda i: (i, 1)),
                  one, one,
                  pl.BlockSpec(ws.shape, lambda i: (0, 0, 0)),
                  pl.BlockSpec((CG_CHUNK, CG_HEADS), lambda i: (0, 0))],
        out_specs=pl.BlockSpec((tl, W), lambda i: (i, 0)),
        out_shape=jax.ShapeDtypeStruct((T, W), BF16),
        compiler_params=_params("parallel"), name="gmlp_gate",
    )(uv, uv, ln_g.reshape(1, W), ln_b.reshape(1, W), ws.astype(BF16), jnp.transpose(bs))


def _row_dmas(n, make_copy):
    assert n % DMA_UNROLL == 0

    def issue(it, carry):
        for u in range(DMA_UNROLL):
            make_copy(it, u).start(priority=u % 2)
        return carry

    def drain(it, carry):
        for u in range(DMA_UNROLL):
            make_copy(it, u).wait()
        return carry

    lax.fori_loop(0, n // DMA_UNROLL, issue, 0)
    lax.fori_loop(0, n // DMA_UNROLL, drain, 0)


def _gather_kernel(nu_ref, tok_ref, f_hbm, o_ref, buf, sem):
    def row_copy(it, u):
        t = it * DMA_UNROLL + u
        return pltpu.make_async_copy(f_hbm.at[pl.ds(tok_ref[0, 0, t], 1), :], buf.at[pl.ds(t, 1), :], sem)

    @pl.when(pl.program_id(0) < nu_ref[0])
    def _():
        _row_dmas(o_ref.shape[0], row_copy)
        o_ref[...] = buf[...].astype(o_ref.dtype)


def _moe_gather(f, row_tok, n_used):
    T, D = f.shape
    P = row_tok.shape[0]
    nb = P // MOE_TM
    grid_spec = pltpu.PrefetchScalarGridSpec(
        num_scalar_prefetch=1, grid=(nb,),
        in_specs=[pl.BlockSpec((1, 1, MOE_TM), lambda i, nu: (i, 0, 0), memory_space=pltpu.SMEM),
                  pl.BlockSpec(memory_space=pl.ANY)],
        out_specs=pl.BlockSpec((MOE_TM, D), lambda i, nu: (i, 0)),
        scratch_shapes=[pltpu.VMEM((MOE_TM, D), F32), pltpu.SemaphoreType.DMA(())])
    return pl.pallas_call(
        _gather_kernel, grid_spec=grid_spec,
        out_shape=jax.ShapeDtypeStruct((P, D), BF16),
        compiler_params=_params("arbitrary"), name="moe_gather",
    )(n_used, row_tok.reshape(nb, 1, MOE_TM), f)


def _expert_changed(be_ref):
    i = pl.program_id(1)
    return jnp.logical_or(i == 0, be_ref[i] != be_ref[jnp.maximum(i - 1, 0)])


def _gu_kernel(be_ref, nu_ref, x_ref, wg_ref, wu_ref, bg_ref, bu_ref, o_ref, wg_bf, wu_bf):
    @pl.when(pl.program_id(1) < nu_ref[0])
    def _():
        @pl.when(_expert_changed(be_ref))
        def _():
            wg_bf[...] = wg_ref[0].astype(BF16)
            wu_bf[...] = wu_ref[0].astype(BF16)

        xv = x_ref[...]
        gv = jnp.dot(xv, wg_bf[...], preferred_element_type=F32) + bg_ref[0]
        uv = jnp.dot(xv, wu_bf[...], preferred_element_type=F32) + bu_ref[0]
        gv = jnp.minimum(gv, SWIGLU_LIMIT)
        uv = jnp.clip(uv, -SWIGLU_LIMIT, SWIGLU_LIMIT)
        act = gv * (1.0 / (1.0 + jnp.exp(-SWIGLU_ALPHA * gv))) * (uv + 1.0)
        o_ref[...] = act.astype(o_ref.dtype)


def _moe_gate_up(xs, block_e, n_used, w_gu, b_gu, *, tn=1024):
    P, D = xs.shape
    F = w_gu.shape[2] // 2
    nj = F // tn
    nb = P // MOE_TM
    b3 = b_gu.reshape(-1, 1, 2 * F)
    grid_spec = pltpu.PrefetchScalarGridSpec(
        num_scalar_prefetch=2, grid=(nj, nb),
        in_specs=[pl.BlockSpec((MOE_TM, D), lambda j, i, be, nu: (i, 0)),
                  pl.BlockSpec((1, D, tn), lambda j, i, be, nu: (be[i], 0, j)),
                  pl.BlockSpec((1, D, tn), lambda j, i, be, nu: (be[i], 0, nj + j)),
                  pl.BlockSpec((1, 1, tn), lambda j, i, be, nu: (be[i], 0, j)),
                  pl.BlockSpec((1, 1, tn), lambda j, i, be, nu: (be[i], 0, nj + j))],
        out_specs=pl.BlockSpec((MOE_TM, tn), lambda j, i, be, nu: (i, j)),
        scratch_shapes=[pltpu.VMEM((D, tn), BF16), pltpu.VMEM((D, tn), BF16)])
    return pl.pallas_call(
        _gu_kernel, grid_spec=grid_spec,
        out_shape=jax.ShapeDtypeStruct((P, F), BF16),
        compiler_params=_params("arbitrary", "arbitrary"), name="moe_gate_up",
    )(block_e, n_used, xs, w_gu, w_gu, b3, b3)


def _down_kernel(be_ref, nu_ref, a_ref, w_ref, b_ref, o_ref, w_bf):
    @pl.when(pl.program_id(1) < nu_ref[0])
    def _():
        @pl.when(_expert_changed(be_ref))
        def _():
            w_bf[...] = w_ref[0].astype(BF16)

        o_ref[...] = jnp.dot(a_ref[...], w_bf[...], preferred_element_type=F32) + b_ref[0]


def _moe_down(act, block_e, n_used, w_down, b_down, *, tn=1024):
    P, F = act.shape
    D = w_down.shape[2]
    nb = P // MOE_TM
    grid_spec = pltpu.PrefetchScalarGridSpec(
        num_scalar_prefetch=2, grid=(D // tn, nb),
        in_specs=[pl.BlockSpec((MOE_TM, F), lambda j, i, be, nu: (i, 0)),
                  pl.BlockSpec((1, F, tn), lambda j, i, be, nu: (be[i], 0, j)),
                  pl.BlockSpec((1, 1, tn), lambda j, i, be, nu: (be[i], 0, j))],
        out_specs=pl.BlockSpec((MOE_TM, tn), lambda j, i, be, nu: (i, j)),
        scratch_shapes=[pltpu.VMEM((F, tn), BF16)])
    return pl.pallas_call(
        _down_kernel, grid_spec=grid_spec,
        out_shape=jax.ShapeDtypeStruct((P, D), F32),
        compiler_params=_params("arbitrary", "arbitrary"), name="moe_down",
    )(block_e, n_used, act, w_down, b_down.reshape(-1, 1, D))


def _combine_kernel(dest_ref, gt_ref, x_ref, g_ref, sc_ref, sh_ref, ys_hbm, *rest, want_x):
    if want_x:
        xo_ref, h_ref, buf, sem = rest
    else:
        h_ref, buf, sem = rest
    tb = x_ref.shape[0]

    def row_copy(it, u):
        t = it * (DMA_UNROLL // TOP_K) + u // TOP_K
        k = u % TOP_K
        return pltpu.make_async_copy(ys_hbm.at[pl.ds(dest_ref[0, 0, t * TOP_K + k], 1), :],
                                     buf.at[k, pl.ds(t, 1), :], sem)

    _row_dmas(tb * TOP_K, row_copy)
    moe = None
    for k in range(TOP_K):
        term = buf[k] * gt_ref[:, k:k + 1]
        moe = term if moe is None else moe + term
    xv = x_ref[...] + g_ref[0] * moe
    if want_x:
        xo_ref[...] = xv
    h_ref[...] = _rms_mod(xv, sc_ref[0], sh_ref[0]).astype(h_ref.dtype)


def _moe_combine(ys, dest, gates, x2, g, sc, sh, L, *, want_x, out_dtype, tb=256):
    T, D = x2.shape
    tb = min(tb, L)
    nl = L // tb
    mod = pl.BlockSpec((1, 1, D), lambda i: (i // nl, 0, 0))
    row = pl.BlockSpec((tb, D), lambda i: (i, 0))
    out_specs = [row, row] if want_x else [row]
    out_shape = ([jax.ShapeDtypeStruct((T, D), F32)] if want_x else []) + [jax.ShapeDtypeStruct((T, D), out_dtype)]
    return pl.pallas_call(
        functools.partial(_combine_kernel, want_x=want_x), grid=(T // tb,),
        in_specs=[pl.BlockSpec((1, 1, tb * TOP_K), lambda i: (i, 0, 0), memory_space=pltpu.SMEM),
                  pl.BlockSpec((tb, TOP_K), lambda i: (i, 0)),
                  row, mod, mod, mod, pl.BlockSpec(memory_space=pl.ANY)],
        out_specs=out_specs, out_shape=out_shape,
        scratch_shapes=[pltpu.VMEM((TOP_K, tb, D), F32), pltpu.SemaphoreType.DMA(())],
        compiler_params=_params("arbitrary"), name="moe_combine",
    )(dest.reshape(T // tb, 1, tb * TOP_K), gates, x2, g, sc, sh, ys)


def _moe_routing(top_i):
    T = top_i.shape[0]
    A = T * TOP_K
    P = A + N_EXPERTS * MOE_TM
    nb = P // MOE_TM
    flat_e = top_i.reshape(A)
    onehot = (flat_e[:, None] == jnp.arange(N_EXPERTS, dtype=jnp.int32)[None, :]).astype(jnp.int32)
    cs = jnp.cumsum(onehot, axis=0)
    rank = jnp.take_along_axis(cs, flat_e[:, None], axis=1)[:, 0] - 1
    counts = cs[-1]
    padded = (counts + MOE_TM - 1) // MOE_TM * MOE_TM
    pad_end = jnp.cumsum(padded)
    pad_start = pad_end - padded
    dest = (pad_start[flat_e] + rank).astype(jnp.int32)
    block_e = jnp.minimum(jnp.searchsorted(pad_end, jnp.arange(nb, dtype=jnp.int32) * MOE_TM, side="right"),
                          N_EXPERTS - 1).astype(jnp.int32)
    n_used = (pad_end[-1:] // MOE_TM).astype(jnp.int32)
    _, sorted_tok = lax.sort((dest, jnp.arange(A, dtype=jnp.int32) // TOP_K), num_keys=1)
    start = jnp.cumsum(counts) - counts
    row = jnp.arange(P, dtype=jnp.int32)
    row_e = jnp.repeat(block_e, MOE_TM)
    off = row - pad_start[row_e]
    src = jnp.clip(start[row_e] + off, 0, A - 1)
    row_tok = jnp.where(off < counts[row_e], sorted_tok[src], 0).astype(jnp.int32)
    return dest, row_tok, block_e, n_used


def _moe(x2, m, g1, sc2, sh2, g2, nsc, nsh, router_w, router_b, w_gu, b_gu, w_down, b_down, L,
         *, layer, want_x, out_dtype):
    xo, f, top_i, gates = _res_norm_router(x2, m, g1, sc2, sh2, jnp.transpose(router_w),
                                           router_b.reshape(N_EXPERTS, 1), L)
    dest, row_tok, block_e, n_used = _moe_routing(top_i)
    xs = _moe_gather(f, row_tok, n_used)
    block_w = block_e + layer * N_EXPERTS
    act = _moe_gate_up(xs, block_w, n_used, w_gu.reshape((-1,) + w_gu.shape[2:]), b_gu)
    ys = _moe_down(act, block_w, n_used, w_down.reshape((-1,) + w_down.shape[2:]), b_down)
    return _moe_combine(ys, dest, gates, xo, g2, nsc, nsh, L, want_x=want_x, out_dtype=out_dtype)


def _ssd_tri():
    l = np.arange(SSD_CHUNK)[:, None]
    j = np.arange(SSD_CHUNK)[None, :]
    return jnp.asarray(np.stack([(j <= l), (j >= l)]).astype(np.float32))


def _ssd_stream(h_bf, w_main, w_dt, conv_w, conv_b, a_log, dt_bias, h0, Bsz, L, width):
    T = h_bf.shape[0]
    H = width // SSD_HEAD_DIM
    p_main = _matmul([(h_bf, w_main)], tm=1024, tn=1024, out_dtype=BF16, name="even_in_proj")
    dt_raw = _matmul([(h_bf, w_dt)], tm=1024, tn=w_dt.shape[1], out_dtype=F32, fold=3, name="dt_proj")
    dt2 = jnp.transpose(dt_raw[:, :2 * H].reshape(T, 2, H), (1, 0, 2))
    xbc = _ssd_conv(p_main, conv_w, conv_b, Bsz, L)
    y, h_last = _ssd_scan(xbc, dt2, _ssd_tri(), a_log, dt_bias, h0, Bsz, L, width)
    return p_main, xbc, y, h_last


def kernel(x, c, ctx, c_ctx, ada_w, ada_b, even_in_w, ssd_conv_w, ssd_conv_b, ssd_a_log, ssd_dt_bias,
           ssd_d, ssd_norm_w, pool_w, pool_b, pool_scale, even_out_w, cg_in_w, cg_ln_g, cg_ln_b, cg_ws,
           cg_bs, cg_out_w, router_w, router_b, moe_w_gu, moe_b_gu, moe_w_down, moe_b_down, final_norm_w):
    Bsz, S, D = x.shape
    Lc = ctx.shape[1]
    T = Bsz * S
    width = D
    H = width // SSD_HEAD_DIM
    gn = SSD_GROUPS * SSD_STATE
    x2 = x.reshape(T, D)
    ctx2 = ctx.reshape(Bsz * Lc, D)

    cin = jnp.zeros((8, D), F32).at[:Bsz].set(c).at[Bsz].set(c_ctx)
    mods = [_matmul([(cin, ada_w[i])], tm=8, tn=1024, out_dtype=F32, bias=ada_b[i], pre_act="silu",
                    precision=HIGHEST, name="ada_mod") for i in range(DEPTH)]

    def lat_mods(i):
        return [mods[i][:Bsz, k * D:(k + 1) * D].reshape(Bsz, 1, D) for k in range(6)]

    sh1, sc1, g1, sh2, sc2, g2 = lat_mods(0)
    sh_c = jnp.broadcast_to(mods[0][Bsz, :D].reshape(1, 1, D), (Bsz, 1, D))
    sc_c = jnp.broadcast_to(mods[0][Bsz, D:2 * D].reshape(1, 1, D), (Bsz, 1, D))
    w_in = even_in_w[0]
    o_dt = 2 * width + 2 * gn
    w_main = jnp.concatenate([w_in[:, :o_dt], w_in[:, o_dt + 2 * H:]], axis=1).astype(BF16)
    w_dt = _split3(jnp.pad(w_in[:, o_dt:o_dt + 2 * H], ((0, 0), (0, 128 - 2 * H))))
    ssd_p = (w_main, w_dt, ssd_conv_w[0], ssd_conv_b[0], ssd_a_log[0], ssd_dt_bias[0])

    h_c = _norm_mod(ctx2, sc_c, sh_c, Lc)
    h0 = jnp.zeros((2, Bsz, SSD_GROUPS, SSD_STATE, width // SSD_GROUPS), F32)
    _, _, _, ctx_state = _ssd_stream(h_c, *ssd_p, h0, Bsz, Lc, width)

    h_l = _norm_mod(x2, sc1, sh1, S)
    p_main, xbc, y, _ = _ssd_stream(h_l, *ssd_p, ctx_state, Bsz, S, width)
    y_ssd = _ssd_out(y, xbc, p_main, ssd_d[0], ssd_norm_w[0])
    y_pool = _pool_mixer(p_main, pool_w[0], pool_b[0], pool_scale[0], Bsz, S)
    w_out = even_out_w[0].astype(BF16)
    m = _matmul([(y_ssd, w_out[:width]), (y_pool, w_out[width:])], tm=1024, tn=1024, out_dtype=F32,
                name="even_out_proj")

    nsh1, nsc1, ng1, nsh2, nsc2, ng2 = lat_mods(1)
    x2, h_l = _moe(x2, m, g1, sc2, sh2, g2, nsc1, nsh1, router_w[0], router_b[0], moe_w_gu, moe_b_gu,
                   moe_w_down, moe_b_down, S, layer=0, want_x=True, out_dtype=BF16)

    uv = _matmul([(h_l, cg_in_w[0].astype(BF16))], tm=1024, tn=1024, out_dtype=BF16, act="gelu",
                 name="cg_in_proj")
    gated = _gmlp_gate(uv, cg_ln_g[0], cg_ln_b[0], cg_ws[0], cg_bs[0])
    m = _matmul([(gated, cg_out_w[0].astype(BF16))], tm=1024, tn=1024, out_dtype=F32, name="cg_out_proj")
    fin_sc = jnp.broadcast_to((final_norm_w - 1.0).reshape(1, 1, D), (Bsz, 1, D))
    fin_sh = jnp.zeros((Bsz, 1, D), F32)
    (out,) = _moe(x2, m, ng1, nsc2, nsh2, ng2, fin_sc, fin_sh, router_w[1], router_b[1], moe_w_gu,
                  moe_b_gu, moe_w_down, moe_b_down, S, layer=1, want_x=False, out_dtype=F32)
    return out.reshape(Bsz, S, D)
```

```python
import functools
import math

import numpy as np
import jax
import jax.numpy as jnp
from jax import lax
from jax.experimental import pallas as pl
from jax.experimental.pallas import tpu as pltpu

F32 = jnp.float32
BF16 = jnp.bfloat16
HIGHEST = lax.Precision.HIGHEST

GRID_W = 64
EPS = 1e-6
DEPTH = 2

SSD_HEAD_DIM = 64
SSD_GROUPS = 8
SSD_STATE = 128
SSD_CONV = 4
SSD_CHUNK = 128

POOL_WINDOWS = (2, 4, 8, 16)
POOL_HALO = 512

CG_HEADS = 8
CG_CHUNK = 128

N_EXPERTS = 32
TOP_K = 4
SWIGLU_ALPHA = 1.702
SWIGLU_LIMIT = 7.0
MOE_TM = 512
F32_SUBLANES = 8

V7X_VMEM_LIMIT_BYTES = 56 * 1024 * 1024
BF16_SUBLANES = 16


def _params(*sem):
    return pltpu.CompilerParams(dimension_semantics=sem, vmem_limit_bytes=V7X_VMEM_LIMIT_BYTES)


def _mm_kernel(*refs, n_pairs, has_bias, pre_act, act, precision, fold):
    o_ref = refs[-1]
    acc = None
    for p in range(n_pairs):
        xv = refs[2 * p][...]
        wv = refs[2 * p + 1][...]
        if precision is not None:
            xv = xv.astype(F32)
        if pre_act == "silu":
            xv = xv * (1.0 / (1.0 + jnp.exp(-xv)))
        d = jnp.dot(xv, wv, preferred_element_type=F32, precision=precision)
        acc = d if acc is None else acc + d
    if has_bias:
        acc = acc + refs[2 * n_pairs][...]
    if act == "gelu":
        acc = 0.5 * acc * (1.0 + lax.erf(acc * (1.0 / math.sqrt(2.0))))
    if fold > 1:
        wn = acc.shape[1] // fold
        acc = sum(acc[:, k * wn:(k + 1) * wn] for k in range(fold))
    o_ref[...] = acc.astype(o_ref.dtype)


def _matmul(pairs, *, tm, tn, out_dtype, name, bias=None, pre_act=None, act=None, precision=None, fold=1):
    M = pairs[0][0].shape[0]
    N = pairs[0][1].shape[1]
    tm = min(tm, M)
    tn = min(tn, N)
    assert M % tm == 0 and N % tn == 0 and (fold == 1 or tn == N)
    in_specs, args = [], []
    for xa, wa in pairs:
        K = xa.shape[1]
        in_specs += [pl.BlockSpec((tm, K), lambda i, j: (i, 0)),
                     pl.BlockSpec((K, tn), lambda i, j: (0, j))]
        args += [xa, wa]
    if bias is not None:
        in_specs.append(pl.BlockSpec((1, tn), lambda i, j: (0, j)))
        args.append(bias.reshape(1, N))
    kern = functools.partial(_mm_kernel, n_pairs=len(pairs), has_bias=bias is not None,
                             pre_act=pre_act, act=act, precision=precision, fold=fold)
    return pl.pallas_call(
        kern, grid=(M // tm, N // tn), in_specs=in_specs,
        out_specs=pl.BlockSpec((tm, tn // fold), lambda i, j: (i, j)),
        out_shape=jax.ShapeDtypeStruct((M, N // fold), out_dtype),
        compiler_params=_params("parallel", "parallel"), name=name)(*args)


def _rms_mod(xv, sc, sh):
    ms = jnp.mean(xv * xv, axis=-1, keepdims=True)
    return xv * lax.rsqrt(ms + EPS) * (1.0 + sc) + sh


def _norm_kernel(x_ref, sc_ref, sh_ref, h_ref):
    h_ref[...] = _rms_mod(x_ref[...], sc_ref[0], sh_ref[0]).astype(h_ref.dtype)


def _norm_mod(x2, sc, sh, L, *, tl=512):
    T, D = x2.shape
    tl = min(tl, L)
    nl = L // tl
    mod = pl.BlockSpec((1, 1, D), lambda i: (i // nl, 0, 0))
    return pl.pallas_call(
        _norm_kernel, grid=(T // tl,),
        in_specs=[pl.BlockSpec((tl, D), lambda i: (i, 0)), mod, mod],
        out_specs=pl.BlockSpec((tl, D), lambda i: (i, 0)),
        out_shape=jax.ShapeDtypeStruct((T, D), BF16),
        compiler_params=_params("parallel"), name="norm_mod")(x2, sc, sh)


def _res_router_kernel(x_ref, m_ref, g_ref, sc_ref, sh_ref, rw_ref, rb_ref,
                       xo_ref, f_ref, ti_ref, tg_ref):
    xv = x_ref[...] + g_ref[0] * m_ref[...]
    xo_ref[...] = xv
    f = _rms_mod(xv, sc_ref[0], sh_ref[0])
    f_ref[...] = f
    lg = lax.dot_general(rw_ref[...], f, (((1,), (1,)), ((), ())),
                         precision=HIGHEST, preferred_element_type=F32) + rb_ref[...]
    iota = lax.broadcasted_iota(jnp.int32, lg.shape, 0)
    vals, idxs = [], []
    for _ in range(TOP_K):
        mx = jnp.max(lg, axis=0, keepdims=True)
        ix = jnp.min(jnp.where(lg == mx, iota, N_EXPERTS), axis=0, keepdims=True)
        vals.append(mx)
        idxs.append(ix)
        lg = jnp.where(iota == ix, -jnp.inf, lg)
    tv = jnp.concatenate(vals, axis=0)
    ex = jnp.exp(tv - vals[0])
    ti_ref[0] = jnp.concatenate(idxs, axis=0)
    tg_ref[0] = ex / jnp.sum(ex, axis=0, keepdims=True)


def _res_norm_router(x2, m, g, sc, sh, rw_t, rb, L, *, tl=512):
    T, D = x2.shape
    tl = min(tl, L)
    nl = L // tl
    nb = T // tl
    mod = pl.BlockSpec((1, 1, D), lambda i: (i // nl, 0, 0))
    row = pl.BlockSpec((tl, D), lambda i: (i, 0))
    tk = pl.BlockSpec((1, TOP_K, tl), lambda i: (i, 0, 0))
    xo, f, ti, tg = pl.pallas_call(
        _res_router_kernel, grid=(nb,),
        in_specs=[row, row, mod, mod, mod,
                  pl.BlockSpec((N_EXPERTS, D), lambda i: (0, 0)),
                  pl.BlockSpec((N_EXPERTS, 1), lambda i: (0, 0))],
        out_specs=[row, row, tk, tk],
        out_shape=[jax.ShapeDtypeStruct((T, D), F32), jax.ShapeDtypeStruct((T, D), F32),
                   jax.ShapeDtypeStruct((nb, TOP_K, tl), jnp.int32),
                   jax.ShapeDtypeStruct((nb, TOP_K, tl), F32)],
        compiler_params=_params("parallel"), name="res_norm_router")(x2, m, g, sc, sh, rw_t, rb)
    top_i = jnp.transpose(ti, (0, 2, 1)).reshape(T, TOP_K)
    gates = jnp.transpose(tg, (0, 2, 1)).reshape(T, TOP_K)
    return xo, f, top_i, gates


def _conv_kernel(prev_ref, main_ref, next_ref, w_ref, b_ref, o_ref):
    i = pl.program_id(1)
    nl = pl.num_programs(1)
    tl = main_ref.shape[0]
    hr = prev_ref.shape[0]
    pv = jnp.where(i > 0, prev_ref[...].astype(F32), 0.0)
    nx = jnp.where(i < nl - 1, next_ref[...].astype(F32), 0.0)
    ext = jnp.concatenate([pv, main_ref[...].astype(F32), nx], axis=0)
    n = ext.shape[0]
    acc = None
    for k in range(SSD_CONV):
        off = k - SSD_CONV // 2
        sh = (-off) % n
        term = (ext if sh == 0 else pltpu.roll(ext, sh, 0))[hr:hr + tl] * w_ref[k:k + 1, :]
        acc = term if acc is None else acc + term
    acc = acc + b_ref[...]
    o_ref[...] = (acc * (1.0 / (1.0 + jnp.exp(-acc)))).astype(o_ref.dtype)


def _ssd_conv(p_main, conv_w, conv_b, Bsz, L, *, tl=512):
    T = p_main.shape[0]
    W = p_main.shape[1] // 4
    tl = min(tl, L)
    nl = L // tl
    hr = BF16_SUBLANES
    lh = L // hr
    th = tl // hr
    return pl.pallas_call(
        _conv_kernel, grid=(Bsz, nl, 2),
        in_specs=[
            pl.BlockSpec((hr, W), lambda b, i, c: (jnp.maximum(b * lh + i * th - 1, b * lh), 1 + c)),
            pl.BlockSpec((tl, W), lambda b, i, c: (b * nl + i, 1 + c)),
            pl.BlockSpec((hr, W), lambda b, i, c: (jnp.minimum(b * lh + (i + 1) * th, (b + 1) * lh - 1), 1 + c)),
            pl.BlockSpec((SSD_CONV, W), lambda b, i, c: (0, c)),
            pl.BlockSpec((1, W), lambda b, i, c: (0, c)),
        ],
        out_specs=pl.BlockSpec((tl, W), lambda b, i, c: (b * nl + i, c)),
        out_shape=jax.ShapeDtypeStruct((T, 2 * W), BF16),
        compiler_params=_params("parallel", "parallel", "parallel"), name="ssd_conv",
    )(p_main, p_main, p_main, conv_w, conv_b.reshape(1, -1))


def _split3(v):
    c1 = v.astype(BF16)
    r1 = v - c1.astype(F32)
    c2 = r1.astype(BF16)
    c3 = (r1 - c2.astype(F32)).astype(BF16)
    return jnp.concatenate([c1, c2, c3], axis=1)


def _scan_kernel(xbc_ref, dt_ref, tri_ref, alog_ref, dtb_ref, eh_ref, es_ref, h0_ref, y_ref, h_ref, *, width):
    c = pl.program_id(2)

    @pl.when(c == 0)
    def _():
        h_ref[...] = h0_ref[...]

    n_heads = width // SSD_HEAD_DIM
    hpg = n_heads // SSD_GROUPS
    gw = hpg * SSD_HEAD_DIM
    gn = SSD_GROUPS * SSD_STATE
    raw = dt_ref[0] + dtb_ref[0]
    dt = jnp.maximum(raw, 0.0) + jnp.log1p(jnp.exp(-jnp.abs(raw)))
    da = dt * (-jnp.exp(alog_ref[0]))
    tri = tri_ref[0]
    cum = jnp.dot(tri, da, precision=HIGHEST, preferred_element_type=F32)
    cum_t = lax.dot_general(da, tri, (((0,), (1,)), ((), ())),
                            precision=HIGHEST, preferred_element_type=F32)
    tot = jnp.broadcast_to(jnp.sum(da, axis=0, keepdims=True), (8, n_heads))
    cum3 = _split3(cum)
    cum_hd = jnp.dot(cum3, eh_ref[...], preferred_element_type=F32)
    dt_hd = jnp.dot(_split3(dt), eh_ref[...], preferred_element_type=F32)
    tot_hd = jnp.dot(_split3(tot), eh_ref[...], preferred_element_type=F32)[0:1]
    cum_st = jnp.dot(cum3, es_ref[...], preferred_element_type=F32)
    e_cum = jnp.exp(cum_hd)
    e_tot = jnp.exp(tot_hd)
    xdt = xbc_ref[:, :width].astype(F32) * dt_hd
    xdt_b = xdt.astype(BF16)
    xdt_end = (xdt * jnp.exp(tot_hd - cum_hd)).astype(BF16)
    mask = tri > 0.0
    first_head = lax.broadcasted_iota(jnp.int32, (SSD_CHUNK, 2 * SSD_HEAD_DIM), 1) < SSD_HEAD_DIM
    zero = jnp.zeros((), BF16)
    for g in range(SSD_GROUPS):
        bg = xbc_ref[:, width + g * SSD_STATE: width + (g + 1) * SSD_STATE]
        cg = xbc_ref[:, width + gn + g * SSD_STATE: width + gn + (g + 1) * SSD_STATE]
        gm = lax.dot_general(cg, bg, (((1,), (1,)), ((), ())), preferred_element_type=F32)
        h_in = h_ref[0, 0, g]
        y_off = jnp.dot(cg, h_in.astype(BF16), preferred_element_type=F32)
        for pair in range(hpg // 2):
            scores = []
            for hh in (2 * pair, 2 * pair + 1):
                h = g * hpg + hh
                seg = cum_st[:, h * SSD_CHUNK:(h + 1) * SSD_CHUNK] - cum_t[h:h + 1, :]
                scores.append((gm * jnp.exp(jnp.where(mask, seg, -jnp.inf))).astype(BF16))
            c0 = g * gw + pair * 2 * SSD_HEAD_DIM
            xp = xdt_b[:, c0:c0 + 2 * SSD_HEAD_DIM]
            rhs = jnp.concatenate([jnp.where(first_head, xp, zero), jnp.where(first_head, zero, xp)], axis=0)
            yp = jnp.dot(jnp.concatenate(scores, axis=1), rhs, preferred_element_type=F32)
            yp = yp + y_off[:, pair * 2 * SSD_HEAD_DIM:(pair + 1) * 2 * SSD_HEAD_DIM] * e_cum[:, c0:c0 + 2 * SSD_HEAD_DIM]
            y_ref[0, :, c0:c0 + 2 * SSD_HEAD_DIM] = yp.astype(y_ref.dtype)
        st = lax.dot_general(bg, xdt_end[:, g * gw:(g + 1) * gw], (((0,), (0,)), ((), ())),
                             preferred_element_type=F32)
        h_ref[0, 0, g] = h_in * e_tot[:, g * gw:(g + 1) * gw] + st


def _head_spread(n_heads, per_head):
    e = np.kron(np.eye(n_heads, dtype=np.float32), np.ones((1, per_head), np.float32))
    return jnp.asarray(np.concatenate([e, e, e], axis=0), dtype=BF16)


def _ssd_scan(xbc, dt2, tri, a_log, dt_bias, h0, Bsz, L, width):
    T = xbc.shape[0]
    H = width // SSD_HEAD_DIM
    nc = L // SSD_CHUNK

    def rows(d, b, c):
        return b * nc + c + d * (nc - 1 - 2 * c)

    st = pl.BlockSpec((1, 1) + h0.shape[2:], lambda d, b, c: (d, b, 0, 0, 0))
    vec = pl.BlockSpec((1, 1, H), lambda d, b, c: (d, 0, 0))
    return pl.pallas_call(
        functools.partial(_scan_kernel, width=width), grid=(2, Bsz, nc),
        in_specs=[
            pl.BlockSpec((SSD_CHUNK, 2 * width), lambda d, b, c: (rows(d, b, c), 0)),
            pl.BlockSpec((1, SSD_CHUNK, H), lambda d, b, c: (d, rows(d, b, c), 0)),
            pl.BlockSpec((1, SSD_CHUNK, SSD_CHUNK), lambda d, b, c: (d, 0, 0)),
            vec, vec,
            pl.BlockSpec((3 * H, width), lambda d, b, c: (0, 0)),
            pl.BlockSpec((3 * H, H * SSD_CHUNK), lambda d, b, c: (0, 0)),
            st,
        ],
        out_specs=[pl.BlockSpec((1, SSD_CHUNK, width), lambda d, b, c: (d, rows(d, b, c), 0)), st],
        out_shape=[jax.ShapeDtypeStruct((2, T, width), BF16),
                   jax.ShapeDtypeStruct(h0.shape, F32)],
        compiler_params=_params("parallel", "parallel", "arbitrary"), name="ssd_scan",
    )(xbc, dt2, tri, a_log.reshape(2, 1, H), dt_bias.reshape(2, 1, H),
      _head_spread(H, SSD_HEAD_DIM), _head_spread(H, SSD_CHUNK), h0)


def _ssd_out_kernel(y_ref, x_ref, z_ref, d_ref, nw_ref, o_ref):
    xs = x_ref[...].astype(F32)
    z = z_ref[...].astype(F32)
    y = y_ref[0].astype(F32) + y_ref[1].astype(F32) + xs * d_ref[...]
    y = y * (z * (1.0 / (1.0 + jnp.exp(-z))))
    gw = y.shape[1] // SSD_GROUPS
    for g in range(SSD_GROUPS):
        seg = y[:, g * gw:(g + 1) * gw]
        ms = jnp.mean(seg * seg, axis=-1, keepdims=True)
        o_ref[:, g * gw:(g + 1) * gw] = (seg * lax.rsqrt(ms + EPS) * nw_ref[:, g * gw:(g + 1) * gw]
                                         ).astype(o_ref.dtype)


def _ssd_out(y, xbc, p_main, d_skip, norm_w, *, tl=512):
    T = xbc.shape[0]
    W = xbc.shape[1] // 2
    tl = min(tl, T)
    row = lambda i: (i, 0)
    one = pl.BlockSpec((1, W), lambda i: (0, 0))
    return pl.pallas_call(
        _ssd_out_kernel, grid=(T // tl,),
        in_specs=[pl.BlockSpec((2, tl, W), lambda i: (0, i, 0)),
                  pl.BlockSpec((tl, W), row), pl.BlockSpec((tl, W), row), one, one],
        out_specs=pl.BlockSpec((tl, W), row),
        out_shape=jax.ShapeDtypeStruct((T, W), BF16),
        compiler_params=_params("parallel"), name="ssd_out",
    )(y, xbc, p_main, jnp.repeat(d_skip, SSD_HEAD_DIM).reshape(1, W), norm_w.reshape(1, W))


def _pool_kernel(prev_ref, v_ref, next_ref, band_ref, pw_ref, pb_ref, ps_ref, o_ref, *, rows):
    i = pl.program_id(1)
    nl = pl.num_programs(1)
    tl = v_ref.shape[0]
    gw = v_ref.shape[1] // len(POOL_WINDOWS)
    tile = band_ref.shape[1]
    pos = lax.broadcasted_iota(jnp.int32, (tl, 128), 0)
    r = i * (tl // GRID_W) + lax.shift_right_logical(pos, GRID_W.bit_length() - 1)
    cidx = lax.bitwise_and(pos, GRID_W - 1)
    for g, w in enumerate(POOL_WINDOWS):
        lo, hi = -(w // 2), w - 1 - w // 2
        cols = slice(g * gw, (g + 1) * gw)
        pv = jnp.where(i > 0, prev_ref[:, cols], jnp.zeros((), BF16))
        nx = jnp.where(i < nl - 1, next_ref[:, cols], jnp.zeros((), BF16))
        vg = v_ref[:, cols]
        ext = jnp.concatenate([pv, vg, nx], axis=0)
        band = band_ref[g]
        cs = jnp.concatenate(
            [jnp.dot(band, ext[t * tile:(t + 1) * tile], preferred_element_type=F32)
             for t in range(ext.shape[0] // tile)], axis=0)
        rs = None
        for o in range(lo, hi + 1):
            s0 = POOL_HALO + GRID_W * o
            term = cs[s0:s0 + tl]
            rs = term if rs is None else rs + term
        n_r = jnp.minimum(r + hi, rows - 1) - jnp.maximum(r + lo, 0) + 1
        n_c = jnp.minimum(cidx + hi, GRID_W - 1) - jnp.maximum(cidx + lo, 0) + 1
        inv = 1.0 / (n_r * n_c).astype(F32)
        inv = jnp.concatenate([inv] * (gw // 128), axis=1)
        d = rs * inv - vg.astype(F32)
        yv = jnp.dot(d.astype(BF16), pw_ref[g], preferred_element_type=F32) + pb_ref[:, cols]
        o_ref[:, cols] = (yv * ps_ref[:, cols]).astype(o_ref.dtype)


def _pool_bands():
    t = 2 * GRID_W
    i = np.arange(t)[:, None]
    j = np.arange(t)[None, :]
    out = []
    for w in POOL_WINDOWS:
        lo, hi = -(w // 2), w - 1 - w // 2
        dc = j % GRID_W - i % GRID_W
        out.append(((i // GRID_W == j // GRID_W) & (dc >= lo) & (dc <= hi)).astype(np.float32))
    return jnp.asarray(np.stack(out), dtype=BF16)


def _pool_mixer(p_main, pool_w, pool_b, pool_scale, Bsz, L, *, tl=1024):
    T = p_main.shape[0]
    W = p_main.shape[1] // 4
    tl = min(tl, L)
    assert L % tl == 0 and tl % POOL_HALO == 0
    nl = L // tl
    lh = L // POOL_HALO
    th = tl // POOL_HALO
    cb = 3
    full = lambda shape: pl.BlockSpec(shape, lambda b, i: (0,) * len(shape))
    return pl.pallas_call(
        functools.partial(_pool_kernel, rows=L // GRID_W), grid=(Bsz, nl),
        in_specs=[
            pl.BlockSpec((POOL_HALO, W), lambda b, i: (jnp.maximum(b * lh + i * th - 1, b * lh), cb)),
            pl.BlockSpec((tl, W), lambda b, i: (b * nl + i, cb)),
            pl.BlockSpec((POOL_HALO, W), lambda b, i: (jnp.minimum(b * lh + (i + 1) * th, (b + 1) * lh - 1), cb)),
            full((len(POOL_WINDOWS), 2 * GRID_W, 2 * GRID_W)),
            full(pool_w.shape), full((1, W)), full((1, W)),
        ],
        out_specs=pl.BlockSpec((tl, W), lambda b, i: (b * nl + i, 0)),
        out_shape=jax.ShapeDtypeStruct((T, W), BF16),
        compiler_params=_params("parallel", "parallel"), name="pool_mixer",
    )(p_main, p_main, p_main, _pool_bands(), pool_w.astype(BF16),
      pool_b.reshape(1, W), pool_scale.reshape(1, W))


def _gmlp_kernel(u_ref, v_ref, g_ref, b_ref, ws_ref, bs_ref, o_ref):
    v = v_ref[...].astype(F32)
    mu = jnp.mean(v, axis=-1, keepdims=True)
    vc = v - mu
    var = jnp.mean(vc * vc, axis=-1, keepdims=True)
    vn = (vc * lax.rsqrt(var + EPS) * g_ref[...] + b_ref[...]).astype(BF16)
    tl, W = v.shape
    hd = W // CG_HEADS
    for n in range(tl // CG_CHUNK):
        rs = slice(n * CG_CHUNK, (n + 1) * CG_CHUNK)
        for h in range(CG_HEADS):
            cs = slice(h * hd, (h + 1) * hd)
            sv = jnp.dot(ws_ref[h], vn[rs, cs], preferred_element_type=F32) + bs_ref[:, h:h + 1]
            o_ref[rs, cs] = (u_ref[rs, cs].astype(F32) * sv).astype(o_ref.dtype)


def _gmlp_gate(uv, ln_g, ln_b, ws, bs, *, tl=256):
    T = uv.shape[0]
    W = uv.shape[1] // 2
    tl = min(tl, T)
    one = pl.BlockSpec((1, W), lambda i: (0, 0))
    return pl.pallas_call(
        _gmlp_kernel, grid=(T // tl,),
        in_specs=[pl.BlockSpec((tl, W), lambda i: (i, 0)), pl.BlockSpec((tl, W), lambda i: (i, 1)),
                  one, one,
                  pl.BlockSpec(ws.shape, lambda i: (0, 0, 0)),
                  pl.BlockSpec((CG_CHUNK, CG_HEADS), lambda i: (0, 0))],
        out_specs=pl.BlockSpec((tl, W), lambda i: (i, 0)),
        out_shape=jax.ShapeDtypeStruct((T, W), BF16),
        compiler_params=_params("parallel"), name="gmlp_gate",
    )(uv, uv, ln_g.reshape(1, W), ln_b.reshape(1, W), ws.astype(BF16), jnp.transpose(bs))


def _start_rows(n, per_iter, make_copy):
    assert n % per_iter == 0

    def issue(it, carry):
        for u in range(per_iter):
            make_copy(it, u).start(priority=u % 2)
        return carry

    lax.fori_loop(0, n // per_iter, issue, 0)


def _wait_rows(n, per_iter, make_copy):
    def drain(it, carry):
        for u in range(per_iter):
            make_copy(it, u).wait()
        return carry

    lax.fori_loop(0, n // per_iter, drain, 0)


def _gather_kernel(nu_ref, tok_ref, nxt_ref, f_hbm, o_ref, buf, sem):
    i = pl.program_id(0)
    tb, D = o_ref.shape
    slot = i % 2

    def row_copy(idx_ref, s):
        def make(it, u):
            return pltpu.make_async_copy(f_hbm.at[pl.ds(idx_ref[0, 0, it * F32_SUBLANES + u], 1), :],
                                         buf.at[s, it, pl.ds(u, 1), :], sem.at[s])
        return make

    @pl.when(jnp.logical_and(i == 0, nu_ref[0] > 0))
    def _():
        _start_rows(tb, F32_SUBLANES, row_copy(tok_ref, 0))

    @pl.when(i + 1 < nu_ref[0])
    def _():
        _start_rows(tb, F32_SUBLANES, row_copy(nxt_ref, 1 - slot))

    @pl.when(i < nu_ref[0])
    def _():
        _wait_rows(tb, F32_SUBLANES, row_copy(tok_ref, slot))
        o_ref[...] = buf[slot].reshape(tb, D).astype(o_ref.dtype)

    @pl.when(i >= nu_ref[0])
    def _():
        o_ref[...] = jnp.zeros(o_ref.shape, o_ref.dtype)


def _moe_gather(f, row_tok, n_used):
    T, D = f.shape
    P = row_tok.shape[0]
    nb = P // MOE_TM
    tok3 = row_tok.reshape(nb, 1, MOE_TM)
    grid_spec = pltpu.PrefetchScalarGridSpec(
        num_scalar_prefetch=1, grid=(nb,),
        in_specs=[pl.BlockSpec((1, 1, MOE_TM), lambda i, nu: (i, 0, 0), memory_space=pltpu.SMEM),
                  pl.BlockSpec((1, 1, MOE_TM), lambda i, nu: (jnp.minimum(i + 1, nb - 1), 0, 0),
                               memory_space=pltpu.SMEM),
                  pl.BlockSpec(memory_space=pl.ANY)],
        out_specs=pl.BlockSpec((MOE_TM, D), lambda i, nu: (i, 0)),
        scratch_shapes=[pltpu.VMEM((2, MOE_TM // F32_SUBLANES, F32_SUBLANES, D), F32),
                        pltpu.SemaphoreType.DMA((2,))])
    return pl.pallas_call(
        _gather_kernel, grid_spec=grid_spec,
        out_shape=jax.ShapeDtypeStruct((P, D), BF16),
        compiler_params=_params("arbitrary"), name="moe_gather",
    )(n_used, tok3, tok3, f)


def _expert_changed(be_ref):
    i = pl.program_id(1)
    return jnp.logical_or(i == 0, be_ref[i] != be_ref[jnp.maximum(i - 1, 0)])


def _gu_kernel(be_ref, nu_ref, x_ref, wg_ref, wu_ref, bg_ref, bu_ref, o_ref, wg_bf, wu_bf):
    @pl.when(pl.program_id(1) < nu_ref[0])
    def _():
        @pl.when(_expert_changed(be_ref))
        def _():
            wg_bf[...] = wg_ref[0].astype(BF16)
            wu_bf[...] = wu_ref[0].astype(BF16)

        xv = x_ref[...]
        gv = jnp.dot(xv, wg_bf[...], preferred_element_type=F32) + bg_ref[0]
        uv = jnp.dot(xv, wu_bf[...], preferred_element_type=F32) + bu_ref[0]
        gv = jnp.minimum(gv, SWIGLU_LIMIT)
        uv = jnp.clip(uv, -SWIGLU_LIMIT, SWIGLU_LIMIT)
        act = gv * (1.0 / (1.0 + jnp.exp(-SWIGLU_ALPHA * gv))) * (uv + 1.0)
        o_ref[...] = act.astype(o_ref.dtype)

    @pl.when(pl.program_id(1) >= nu_ref[0])
    def _():
        o_ref[...] = jnp.zeros(o_ref.shape, o_ref.dtype)


def _moe_gate_up(xs, block_e, n_used, w_gu, b_gu, *, tn=1024):
    P, D = xs.shape
    F = w_gu.shape[2] // 2
    nj = F // tn
    nb = P // MOE_TM
    b3 = b_gu.reshape(-1, 1, 2 * F)
    grid_spec = pltpu.PrefetchScalarGridSpec(
        num_scalar_prefetch=2, grid=(nj, nb),
        in_specs=[pl.BlockSpec((MOE_TM, D), lambda j, i, be, nu: (i, 0)),
                  pl.BlockSpec((1, D, tn), lambda j, i, be, nu: (be[i], 0, j)),
                  pl.BlockSpec((1, D, tn), lambda j, i, be, nu: (be[i], 0, nj + j)),
                  pl.BlockSpec((1, 1, tn), lambda j, i, be, nu: (be[i], 0, j)),
                  pl.BlockSpec((1, 1, tn), lambda j, i, be, nu: (be[i], 0, nj + j))],
        out_specs=pl.BlockSpec((MOE_TM, tn), lambda j, i, be, nu: (i, j)),
        scratch_shapes=[pltpu.VMEM((D, tn), BF16), pltpu.VMEM((D, tn), BF16)])
    return pl.pallas_call(
        _gu_kernel, grid_spec=grid_spec,
        out_shape=jax.ShapeDtypeStruct((P, F), BF16),
        compiler_params=_params("arbitrary", "arbitrary"), name="moe_gate_up",
    )(block_e, n_used, xs, w_gu, w_gu, b3, b3)


def _down_kernel(be_ref, nu_ref, a_ref, w_ref, b_ref, o_ref, w_bf):
    @pl.when(pl.program_id(1) < nu_ref[0])
    def _():
        @pl.when(_expert_changed(be_ref))
        def _():
            w_bf[...] = w_ref[0].astype(BF16)

        o_ref[...] = jnp.dot(a_ref[...], w_bf[...], preferred_element_type=F32) + b_ref[0]

    @pl.when(pl.program_id(1) >= nu_ref[0])
    def _():
        o_ref[...] = jnp.zeros(o_ref.shape, o_ref.dtype)


def _moe_down(act, block_e, n_used, w_down, b_down, *, tn=1024):
    P, F = act.shape
    D = w_down.shape[2]
    nb = P // MOE_TM
    grid_spec = pltpu.PrefetchScalarGridSpec(
        num_scalar_prefetch=2, grid=(D // tn, nb),
        in_specs=[pl.BlockSpec((MOE_TM, F), lambda j, i, be, nu: (i, 0)),
                  pl.BlockSpec((1, F, tn), lambda j, i, be, nu: (be[i], 0, j)),
                  pl.BlockSpec((1, 1, tn), lambda j, i, be, nu: (be[i], 0, j))],
        out_specs=pl.BlockSpec((MOE_TM, tn), lambda j, i, be, nu: (i, j)),
        scratch_shapes=[pltpu.VMEM((F, tn), BF16)])
    return pl.pallas_call(
        _down_kernel, grid_spec=grid_spec,
        out_shape=jax.ShapeDtypeStruct((P, D), F32),
        compiler_params=_params("arbitrary", "arbitrary"), name="moe_down",
    )(block_e, n_used, act, w_down, b_down.reshape(-1, 1, D))


def _combine_kernel(dest_ref, nxt_ref, gt_ref, x_ref, g_ref, sc_ref, sh_ref, ys_hbm, *rest, want_x):
    if want_x:
        xo_ref, h_ref, buf, sem = rest
    else:
        h_ref, buf, sem = rest
    i = pl.program_id(0)
    tb, D = x_ref.shape
    per_iter = F32_SUBLANES * TOP_K
    slot = i % 2

    def row_copy(idx_ref, s):
        def make(it, u):
            return pltpu.make_async_copy(ys_hbm.at[pl.ds(idx_ref[0, 0, it * per_iter + u], 1), :],
                                         buf.at[s, u % TOP_K, it, pl.ds(u // TOP_K, 1), :], sem.at[s])
        return make

    @pl.when(i == 0)
    def _():
        _start_rows(tb * TOP_K, per_iter, row_copy(dest_ref, 0))

    @pl.when(i + 1 < pl.num_programs(0))
    def _():
        _start_rows(tb * TOP_K, per_iter, row_copy(nxt_ref, 1 - slot))

    _wait_rows(tb * TOP_K, per_iter, row_copy(dest_ref, slot))
    moe = None
    for k in range(TOP_K):
        term = buf[slot, k].reshape(tb, D) * gt_ref[:, k:k + 1]
        moe = term if moe is None else moe + term
    xv = x_ref[...] + g_ref[0] * moe
    if want_x:
        xo_ref[...] = xv
    h_ref[...] = _rms_mod(xv, sc_ref[0], sh_ref[0]).astype(h_ref.dtype)


def _moe_combine(ys, dest, gates, x2, g, sc, sh, L, *, want_x, out_dtype, tb=256):
    T, D = x2.shape
    tb = min(tb, L)
    nl = L // tb
    mod = pl.BlockSpec((1, 1, D), lambda i: (i // nl, 0, 0))
    row = pl.BlockSpec((tb, D), lambda i: (i, 0))
    out_specs = [row, row] if want_x else [row]
    out_shape = ([jax.ShapeDtypeStruct((T, D), F32)] if want_x else []) + [jax.ShapeDtypeStruct((T, D), out_dtype)]
    nb = T // tb
    dest3 = dest.reshape(nb, 1, tb * TOP_K)
    return pl.pallas_call(
        functools.partial(_combine_kernel, want_x=want_x), grid=(nb,),
        in_specs=[pl.BlockSpec((1, 1, tb * TOP_K), lambda i: (i, 0, 0), memory_space=pltpu.SMEM),
                  pl.BlockSpec((1, 1, tb * TOP_K), lambda i: (jnp.minimum(i + 1, nb - 1), 0, 0),
                               memory_space=pltpu.SMEM),
                  pl.BlockSpec((tb, TOP_K), lambda i: (i, 0)),
                  row, mod, mod, mod, pl.BlockSpec(memory_space=pl.ANY)],
        out_specs=out_specs, out_shape=out_shape,
        scratch_shapes=[pltpu.VMEM((2, TOP_K, tb // F32_SUBLANES, F32_SUBLANES, D), F32),
                        pltpu.SemaphoreType.DMA((2,))],
        compiler_params=_params("arbitrary"), name="moe_combine",
    )(dest3, dest3, gates, x2, g, sc, sh, ys)


def _moe_routing(top_i):
    T = top_i.shape[0]
    A = T * TOP_K
    P = A + N_EXPERTS * MOE_TM
    nb = P // MOE_TM
    flat_e = top_i.reshape(A)
    onehot = (flat_e[:, None] == jnp.arange(N_EXPERTS, dtype=jnp.int32)[None, :]).astype(jnp.int32)
    cs = jnp.cumsum(onehot, axis=0)
    rank = jnp.take_along_axis(cs, flat_e[:, None], axis=1)[:, 0] - 1
    counts = cs[-1]
    padded = (counts + MOE_TM - 1) // MOE_TM * MOE_TM
    pad_end = jnp.cumsum(padded)
    pad_start = pad_end - padded
    dest = (pad_start[flat_e] + rank).astype(jnp.int32)
    block_e = jnp.minimum(jnp.searchsorted(pad_end, jnp.arange(nb, dtype=jnp.int32) * MOE_TM, side="right"),
                          N_EXPERTS - 1).astype(jnp.int32)
    n_used = (pad_end[-1:] // MOE_TM).astype(jnp.int32)
    _, sorted_tok = lax.sort((dest, jnp.arange(A, dtype=jnp.int32) // TOP_K), num_keys=1)
    start = jnp.cumsum(counts) - counts
    row = jnp.arange(P, dtype=jnp.int32)
    row_e = jnp.repeat(block_e, MOE_TM)
    off = row - pad_start[row_e]
    src = jnp.clip(start[row_e] + off, 0, A - 1)
    row_tok = jnp.where(off < counts[row_e], sorted_tok[src], 0).astype(jnp.int32)
    return dest, row_tok, block_e, n_used


def _moe(x2, m, g1, sc2, sh2, g2, nsc, nsh, router_w, router_b, w_gu, b_gu, w_down, b_down, L,
         *, layer, want_x, out_dtype):
    xo, f, top_i, gates = _res_norm_router(x2, m, g1, sc2, sh2, jnp.transpose(router_w),
                                           router_b.reshape(N_EXPERTS, 1), L)
    dest, row_tok, block_e, n_used = _moe_routing(top_i)
    xs = _moe_gather(f, row_tok, n_used)
    block_w = block_e + layer * N_EXPERTS
    act = _moe_gate_up(xs, block_w, n_used, w_gu.reshape((-1,) + w_gu.shape[2:]), b_gu)
    ys = _moe_down(act, block_w, n_used, w_down.reshape((-1,) + w_down.shape[2:]), b_down)
    return _moe_combine(ys, dest, gates, xo, g2, nsc, nsh, L, want_x=want_x, out_dtype=out_dtype)


def _ssd_tri():
    l = np.arange(SSD_CHUNK)[:, None]
    j = np.arange(SSD_CHUNK)[None, :]
    return jnp.asarray(np.stack([(j <= l), (j >= l)]).astype(np.float32))


def _ssd_stream(h_bf, w_main, w_dt, conv_w, conv_b, a_log, dt_bias, h0, Bsz, L, width):
    T = h_bf.shape[0]
    H = width // SSD_HEAD_DIM
    p_main = _matmul([(h_bf, w_main)], tm=1024, tn=1024, out_dtype=BF16, name="even_in_proj")
    dt_raw = _matmul([(h_bf, w_dt)], tm=1024, tn=w_dt.shape[1], out_dtype=F32, fold=3, name="dt_proj")
    dt2 = jnp.transpose(dt_raw[:, :2 * H].reshape(T, 2, H), (1, 0, 2))
    xbc = _ssd_conv(p_main, conv_w, conv_b, Bsz, L)
    y, h_last = _ssd_scan(xbc, dt2, _ssd_tri(), a_log, dt_bias, h0, Bsz, L, width)
    return p_main, xbc, y, h_last


def kernel(x, c, ctx, c_ctx, ada_w, ada_b, even_in_w, ssd_conv_w, ssd_conv_b, ssd_a_log, ssd_dt_bias,
           ssd_d, ssd_norm_w, pool_w, pool_b, pool_scale, even_out_w, cg_in_w, cg_ln_g, cg_ln_b, cg_ws,
           cg_bs, cg_out_w, router_w, router_b, moe_w_gu, moe_b_gu, moe_w_down, moe_b_down, final_norm_w):
    Bsz, S, D = x.shape
    Lc = ctx.shape[1]
    T = Bsz * S
    width = D
    H = width // SSD_HEAD_DIM
    gn = SSD_GROUPS * SSD_STATE
    x2 = x.reshape(T, D)
    ctx2 = ctx.reshape(Bsz * Lc, D)

    cin = jnp.zeros((8, D), F32).at[:Bsz].set(c).at[Bsz].set(c_ctx)
    mods = [_matmul([(cin, ada_w[i])], tm=8, tn=1024, out_dtype=F32, bias=ada_b[i], pre_act="silu",
                    precision=HIGHEST, name="ada_mod") for i in range(DEPTH)]

    def lat_mods(i):
        return [mods[i][:Bsz, k * D:(k + 1) * D].reshape(Bsz, 1, D) for k in range(6)]

    sh1, sc1, g1, sh2, sc2, g2 = lat_mods(0)
    sh_c = jnp.broadcast_to(mods[0][Bsz, :D].reshape(1, 1, D), (Bsz, 1, D))
    sc_c = jnp.broadcast_to(mods[0][Bsz, D:2 * D].reshape(1, 1, D), (Bsz, 1, D))
    w_in = even_in_w[0]
    o_dt = 2 * width + 2 * gn
    w_main = jnp.concatenate([w_in[:, :o_dt], w_in[:, o_dt + 2 * H:]], axis=1).astype(BF16)
    w_dt = _split3(jnp.pad(w_in[:, o_dt:o_dt + 2 * H], ((0, 0), (0, 128 - 2 * H))))
    ssd_p = (w_main, w_dt, ssd_conv_w[0], ssd_conv_b[0], ssd_a_log[0], ssd_dt_bias[0])

    h_c = _norm_mod(ctx2, sc_c, sh_c, Lc)
    h0 = jnp.zeros((2, Bsz, SSD_GROUPS, SSD_STATE, width // SSD_GROUPS), F32)
    _, _, _, ctx_state = _ssd_stream(h_c, *ssd_p, h0, Bsz, Lc, width)

    h_l = _norm_mod(x2, sc1, sh1, S)
    p_main, xbc, y, _ = _ssd_stream(h_l, *ssd_p, ctx_state, Bsz, S, width)
    y_ssd = _ssd_out(y, xbc, p_main, ssd_d[0], ssd_norm_w[0])
    y_pool = _pool_mixer(p_main, pool_w[0], pool_b[0], pool_scale[0], Bsz, S)
    w_out = even_out_w[0].astype(BF16)
    m = _matmul([(y_ssd, w_out[:width]), (y_pool, w_out[width:])], tm=1024, tn=1024, out_dtype=F32,
                name="even_out_proj")

    nsh1, nsc1, ng1, nsh2, nsc2, ng2 = lat_mods(1)
    x2, h_l = _moe(x2, m, g1, sc2, sh2, g2, nsc1, nsh1, router_w[0], router_b[0], moe_w_gu, moe_b_gu,
                   moe_w_down, moe_b_down, S, layer=0, want_x=True, out_dtype=BF16)

    uv = _matmul([(h_l, cg_in_w[0].astype(BF16))], tm=1024, tn=1024, out_dtype=BF16, act="gelu",
                 name="cg_in_proj")
    gated = _gmlp_gate(uv, cg_ln_g[0], cg_ln_b[0], cg_ws[0], cg_bs[0])
    m = _matmul([(gated, cg_out_w[0].astype(BF16))], tm=1024, tn=1024, out_dtype=F32, name="cg_out_proj")
    fin_sc = jnp.broadcast_to((final_norm_w - 1.0).reshape(1, 1, D), (Bsz, 1, D))
    fin_sh = jnp.zeros((Bsz, 1, D), F32)
    (out,) = _moe(x2, m, ng1, nsc2, nsh2, ng2, fin_sc, fin_sh, router_w[1], router_b[1], moe_w_gu,
                  moe_b_gu, moe_w_down, moe_b_down, S, layer=1, want_x=False, out_dtype=F32)
    return out.reshape(Bsz, S, D)
```

```python
import functools
import math

import numpy as np
import jax
import jax.numpy as jnp
from jax import lax
from jax.experimental import pallas as pl
from jax.experimental.pallas import tpu as pltpu

F32 = jnp.float32
BF16 = jnp.bfloat16
HIGHEST = lax.Precision.HIGHEST

GRID_W = 64
EPS = 1e-6
DEPTH = 2

SSD_HEAD_DIM = 64
SSD_GROUPS = 8
SSD_STATE = 128
SSD_CONV = 4
SSD_CHUNK = 128

POOL_WINDOWS = (2, 4, 8, 16)
POOL_HALO = 512

CG_HEADS = 8
CG_CHUNK = 128

N_EXPERTS = 32
TOP_K = 4
SWIGLU_ALPHA = 1.702
SWIGLU_LIMIT = 7.0
MOE_TM = 512
LANES = 128
F32_SUBLANES = 8

V7X_VMEM_LIMIT_BYTES = 56 * 1024 * 1024
BF16_SUBLANES = 16


def _params(*sem):
    return pltpu.CompilerParams(dimension_semantics=sem, vmem_limit_bytes=V7X_VMEM_LIMIT_BYTES)


def _mm_kernel(*refs, n_pairs, has_bias, pre_act, act, precision, fold):
    o_ref = refs[-1]
    acc = None
    for p in range(n_pairs):
        xv = refs[2 * p][...]
        wv = refs[2 * p + 1][...]
        if precision is not None:
            xv = xv.astype(F32)
        if pre_act == "silu":
            xv = xv * (1.0 / (1.0 + jnp.exp(-xv)))
        d = jnp.dot(xv, wv, preferred_element_type=F32, precision=precision)
        acc = d if acc is None else acc + d
    if has_bias:
        acc = acc + refs[2 * n_pairs][...]
    if act == "gelu":
        acc = 0.5 * acc * (1.0 + lax.erf(acc * (1.0 / math.sqrt(2.0))))
    if fold > 1:
        wn = acc.shape[1] // fold
        acc = sum(acc[:, k * wn:(k + 1) * wn] for k in range(fold))
    o_ref[...] = acc.astype(o_ref.dtype)


def _matmul(pairs, *, tm, tn, out_dtype, name, bias=None, pre_act=None, act=None, precision=None, fold=1):
    M = pairs[0][0].shape[0]
    N = pairs[0][1].shape[1]
    tm = min(tm, M)
    tn = min(tn, N)
    assert M % tm == 0 and N % tn == 0 and (fold == 1 or tn == N)
    in_specs, args = [], []
    for xa, wa in pairs:
        K = xa.shape[1]
        in_specs += [pl.BlockSpec((tm, K), lambda i, j: (i, 0)),
                     pl.BlockSpec((K, tn), lambda i, j: (0, j))]
        args += [xa, wa]
    if bias is not None:
        in_specs.append(pl.BlockSpec((1, tn), lambda i, j: (0, j)))
        args.append(bias.reshape(1, N))
    kern = functools.partial(_mm_kernel, n_pairs=len(pairs), has_bias=bias is not None,
                             pre_act=pre_act, act=act, precision=precision, fold=fold)
    return pl.pallas_call(
        kern, grid=(M // tm, N // tn), in_specs=in_specs,
        out_specs=pl.BlockSpec((tm, tn // fold), lambda i, j: (i, j)),
        out_shape=jax.ShapeDtypeStruct((M, N // fold), out_dtype),
        compiler_params=_params("parallel", "parallel"), name=name)(*args)


def _rms_mod(xv, sc, sh):
    ms = jnp.mean(xv * xv, axis=-1, keepdims=True)
    return xv * lax.rsqrt(ms + EPS) * (1.0 + sc) + sh


def _norm_kernel(x_ref, sc_ref, sh_ref, h_ref):
    h_ref[...] = _rms_mod(x_ref[...], sc_ref[0], sh_ref[0]).astype(h_ref.dtype)


def _norm_mod(x2, sc, sh, L, *, tl=512):
    T, D = x2.shape
    tl = min(tl, L)
    nl = L // tl
    mod = pl.BlockSpec((1, 1, D), lambda i: (i // nl, 0, 0))
    return pl.pallas_call(
        _norm_kernel, grid=(T // tl,),
        in_specs=[pl.BlockSpec((tl, D), lambda i: (i, 0)), mod, mod],
        out_specs=pl.BlockSpec((tl, D), lambda i: (i, 0)),
        out_shape=jax.ShapeDtypeStruct((T, D), BF16),
        compiler_params=_params("parallel"), name="norm_mod")(x2, sc, sh)


def _res_router_kernel(x_ref, m_ref, g_ref, sc_ref, sh_ref, rw_ref, rb_ref,
                       xo_ref, f_ref, ti_ref, tg_ref):
    xv = x_ref[...] + g_ref[0] * m_ref[...]
    xo_ref[...] = xv
    f = _rms_mod(xv, sc_ref[0], sh_ref[0])
    tl, D = f.shape
    nch = D // LANES
    for ch in range(nch):
        f_ref[pl.ds(ch, tl, stride=nch), :] = f[:, ch * LANES:(ch + 1) * LANES]
    lg = lax.dot_general(rw_ref[...], f, (((1,), (1,)), ((), ())),
                         precision=HIGHEST, preferred_element_type=F32) + rb_ref[...]
    iota = lax.broadcasted_iota(jnp.int32, lg.shape, 0)
    vals, idxs = [], []
    for _ in range(TOP_K):
        mx = jnp.max(lg, axis=0, keepdims=True)
        ix = jnp.min(jnp.where(lg == mx, iota, N_EXPERTS), axis=0, keepdims=True)
        vals.append(mx)
        idxs.append(ix)
        lg = jnp.where(iota == ix, -jnp.inf, lg)
    tv = jnp.concatenate(vals, axis=0)
    ex = jnp.exp(tv - vals[0])
    ti_ref[0] = jnp.concatenate(idxs, axis=0)
    tg_ref[0] = ex / jnp.sum(ex, axis=0, keepdims=True)


def _res_norm_router(x2, m, g, sc, sh, rw_t, rb, L, *, tl=512):
    T, D = x2.shape
    tl = min(tl, L)
    nl = L // tl
    nb = T // tl
    mod = pl.BlockSpec((1, 1, D), lambda i: (i // nl, 0, 0))
    row = pl.BlockSpec((tl, D), lambda i: (i, 0))
    tk = pl.BlockSpec((1, TOP_K, tl), lambda i: (i, 0, 0))
    xo, f, ti, tg = pl.pallas_call(
        _res_router_kernel, grid=(nb,),
        in_specs=[row, row, mod, mod, mod,
                  pl.BlockSpec((N_EXPERTS, D), lambda i: (0, 0)),
                  pl.BlockSpec((N_EXPERTS, 1), lambda i: (0, 0))],
        out_specs=[row, pl.BlockSpec((tl * (D // LANES), LANES), lambda i: (i, 0)), tk, tk],
        out_shape=[jax.ShapeDtypeStruct((T, D), F32), jax.ShapeDtypeStruct((T * (D // LANES), LANES), F32),
                   jax.ShapeDtypeStruct((nb, TOP_K, tl), jnp.int32),
                   jax.ShapeDtypeStruct((nb, TOP_K, tl), F32)],
        compiler_params=_params("parallel"), name="res_norm_router")(x2, m, g, sc, sh, rw_t, rb)
    top_i = jnp.transpose(ti, (0, 2, 1)).reshape(T, TOP_K)
    gates = jnp.transpose(tg, (0, 2, 1)).reshape(T, TOP_K)
    return xo, f, top_i, gates


def _conv_kernel(prev_ref, main_ref, next_ref, w_ref, b_ref, o_ref):
    i = pl.program_id(1)
    nl = pl.num_programs(1)
    tl = main_ref.shape[0]
    hr = prev_ref.shape[0]
    pv = jnp.where(i > 0, prev_ref[...].astype(F32), 0.0)
    nx = jnp.where(i < nl - 1, next_ref[...].astype(F32), 0.0)
    ext = jnp.concatenate([pv, main_ref[...].astype(F32), nx], axis=0)
    n = ext.shape[0]
    acc = None
    for k in range(SSD_CONV):
        off = k - SSD_CONV // 2
        sh = (-off) % n
        term = (ext if sh == 0 else pltpu.roll(ext, sh, 0))[hr:hr + tl] * w_ref[k:k + 1, :]
        acc = term if acc is None else acc + term
    acc = acc + b_ref[...]
    o_ref[...] = (acc * (1.0 / (1.0 + jnp.exp(-acc)))).astype(o_ref.dtype)


def _ssd_conv(p_main, conv_w, conv_b, Bsz, L, *, tl=512):
    T = p_main.shape[0]
    W = p_main.shape[1] // 4
    tl = min(tl, L)
    nl = L // tl
    hr = BF16_SUBLANES
    lh = L // hr
    th = tl // hr
    return pl.pallas_call(
        _conv_kernel, grid=(Bsz, nl, 2),
        in_specs=[
            pl.BlockSpec((hr, W), lambda b, i, c: (jnp.maximum(b * lh + i * th - 1, b * lh), 1 + c)),
            pl.BlockSpec((tl, W), lambda b, i, c: (b * nl + i, 1 + c)),
            pl.BlockSpec((hr, W), lambda b, i, c: (jnp.minimum(b * lh + (i + 1) * th, (b + 1) * lh - 1), 1 + c)),
            pl.BlockSpec((SSD_CONV, W), lambda b, i, c: (0, c)),
            pl.BlockSpec((1, W), lambda b, i, c: (0, c)),
        ],
        out_specs=pl.BlockSpec((tl, W), lambda b, i, c: (b * nl + i, c)),
        out_shape=jax.ShapeDtypeStruct((T, 2 * W), BF16),
        compiler_params=_params("parallel", "parallel", "parallel"), name="ssd_conv",
    )(p_main, p_main, p_main, conv_w, conv_b.reshape(1, -1))


def _split3(v):
    c1 = v.astype(BF16)
    r1 = v - c1.astype(F32)
    c2 = r1.astype(BF16)
    c3 = (r1 - c2.astype(F32)).astype(BF16)
    return jnp.concatenate([c1, c2, c3], axis=1)


def _scan_kernel(xbc_ref, dt_ref, tri_ref, alog_ref, dtb_ref, eh_ref, es_ref, h0_ref, y_ref, h_ref, *, width):
    c = pl.program_id(2)

    @pl.when(c == 0)
    def _():
        h_ref[...] = h0_ref[...]

    n_heads = width // SSD_HEAD_DIM
    hpg = n_heads // SSD_GROUPS
    gw = hpg * SSD_HEAD_DIM
    gn = SSD_GROUPS * SSD_STATE
    raw = dt_ref[0] + dtb_ref[0]
    dt = jnp.maximum(raw, 0.0) + jnp.log1p(jnp.exp(-jnp.abs(raw)))
    da = dt * (-jnp.exp(alog_ref[0]))
    tri = tri_ref[0]
    cum = jnp.dot(tri, da, precision=HIGHEST, preferred_element_type=F32)
    cum_t = lax.dot_general(da, tri, (((0,), (1,)), ((), ())),
                            precision=HIGHEST, preferred_element_type=F32)
    tot = jnp.broadcast_to(jnp.sum(da, axis=0, keepdims=True), (8, n_heads))
    cum3 = _split3(cum)
    cum_hd = jnp.dot(cum3, eh_ref[...], preferred_element_type=F32)
    dt_hd = jnp.dot(_split3(dt), eh_ref[...], preferred_element_type=F32)
    tot_hd = jnp.dot(_split3(tot), eh_ref[...], preferred_element_type=F32)[0:1]
    cum_st = jnp.dot(cum3, es_ref[...], preferred_element_type=F32)
    e_cum = jnp.exp(cum_hd)
    e_tot = jnp.exp(tot_hd)
    xdt = xbc_ref[:, :width].astype(F32) * dt_hd
    xdt_b = xdt.astype(BF16)
    xdt_end = (xdt * jnp.exp(tot_hd - cum_hd)).astype(BF16)
    mask = tri > 0.0
    first_head = lax.broadcasted_iota(jnp.int32, (SSD_CHUNK, 2 * SSD_HEAD_DIM), 1) < SSD_HEAD_DIM
    zero = jnp.zeros((), BF16)
    for g in range(SSD_GROUPS):
        bg = xbc_ref[:, width + g * SSD_STATE: width + (g + 1) * SSD_STATE]
        cg = xbc_ref[:, width + gn + g * SSD_STATE: width + gn + (g + 1) * SSD_STATE]
        gm = lax.dot_general(cg, bg, (((1,), (1,)), ((), ())), preferred_element_type=F32)
        h_in = h_ref[0, 0, g]
        y_off = jnp.dot(cg, h_in.astype(BF16), preferred_element_type=F32)
        for pair in range(hpg // 2):
            scores = []
            for hh in (2 * pair, 2 * pair + 1):
                h = g * hpg + hh
                seg = cum_st[:, h * SSD_CHUNK:(h + 1) * SSD_CHUNK] - cum_t[h:h + 1, :]
                scores.append((gm * jnp.exp(jnp.where(mask, seg, -jnp.inf))).astype(BF16))
            c0 = g * gw + pair * 2 * SSD_HEAD_DIM
            xp = xdt_b[:, c0:c0 + 2 * SSD_HEAD_DIM]
            rhs = jnp.concatenate([jnp.where(first_head, xp, zero), jnp.where(first_head, zero, xp)], axis=0)
            yp = jnp.dot(jnp.concatenate(scores, axis=1), rhs, preferred_element_type=F32)
            yp = yp + y_off[:, pair * 2 * SSD_HEAD_DIM:(pair + 1) * 2 * SSD_HEAD_DIM] * e_cum[:, c0:c0 + 2 * SSD_HEAD_DIM]
            y_ref[0, :, c0:c0 + 2 * SSD_HEAD_DIM] = yp.astype(y_ref.dtype)
        st = lax.dot_general(bg, xdt_end[:, g * gw:(g + 1) * gw], (((0,), (0,)), ((), ())),
                             preferred_element_type=F32)
        h_ref[0, 0, g] = h_in * e_tot[:, g * gw:(g + 1) * gw] + st


def _head_spread(n_heads, per_head):
    e = np.kron(np.eye(n_heads, dtype=np.float32), np.ones((1, per_head), np.float32))
    return jnp.asarray(np.concatenate([e, e, e], axis=0), dtype=BF16)


def _ssd_scan(xbc, dt2, tri, a_log, dt_bias, h0, Bsz, L, width):
    T = xbc.shape[0]
    H = width // SSD_HEAD_DIM
    nc = L // SSD_CHUNK

    def rows(d, b, c):
        return b * nc + c + d * (nc - 1 - 2 * c)

    st = pl.BlockSpec((1, 1) + h0.shape[2:], lambda d, b, c: (d, b, 0, 0, 0))
    vec = pl.BlockSpec((1, 1, H), lambda d, b, c: (d, 0, 0))
    return pl.pallas_call(
        functools.partial(_scan_kernel, width=width), grid=(2, Bsz, nc),
        in_specs=[
            pl.BlockSpec((SSD_CHUNK, 2 * width), lambda d, b, c: (rows(d, b, c), 0)),
            pl.BlockSpec((1, SSD_CHUNK, H), lambda d, b, c: (d, rows(d, b, c), 0)),
            pl.BlockSpec((1, SSD_CHUNK, SSD_CHUNK), lambda d, b, c: (d, 0, 0)),
            vec, vec,
            pl.BlockSpec((3 * H, width), lambda d, b, c: (0, 0)),
            pl.BlockSpec((3 * H, H * SSD_CHUNK), lambda d, b, c: (0, 0)),
            st,
        ],
        out_specs=[pl.BlockSpec((1, SSD_CHUNK, width), lambda d, b, c: (d, rows(d, b, c), 0)), st],
        out_shape=[jax.ShapeDtypeStruct((2, T, width), BF16),
                   jax.ShapeDtypeStruct(h0.shape, F32)],
        compiler_params=_params("parallel", "parallel", "arbitrary"), name="ssd_scan",
    )(xbc, dt2, tri, a_log.reshape(2, 1, H), dt_bias.reshape(2, 1, H),
      _head_spread(H, SSD_HEAD_DIM), _head_spread(H, SSD_CHUNK), h0)


def _ssd_out_kernel(y_ref, x_ref, z_ref, d_ref, nw_ref, o_ref):
    xs = x_ref[...].astype(F32)
    z = z_ref[...].astype(F32)
    y = y_ref[0].astype(F32) + y_ref[1].astype(F32) + xs * d_ref[...]
    y = y * (z * (1.0 / (1.0 + jnp.exp(-z))))
    gw = y.shape[1] // SSD_GROUPS
    for g in range(SSD_GROUPS):
        seg = y[:, g * gw:(g + 1) * gw]
        ms = jnp.mean(seg * seg, axis=-1, keepdims=True)
        o_ref[:, g * gw:(g + 1) * gw] = (seg * lax.rsqrt(ms + EPS) * nw_ref[:, g * gw:(g + 1) * gw]
                                         ).astype(o_ref.dtype)


def _ssd_out(y, xbc, p_main, d_skip, norm_w, *, tl=512):
    T = xbc.shape[0]
    W = xbc.shape[1] // 2
    tl = min(tl, T)
    row = lambda i: (i, 0)
    one = pl.BlockSpec((1, W), lambda i: (0, 0))
    return pl.pallas_call(
        _ssd_out_kernel, grid=(T // tl,),
        in_specs=[pl.BlockSpec((2, tl, W), lambda i: (0, i, 0)),
                  pl.BlockSpec((tl, W), row), pl.BlockSpec((tl, W), row), one, one],
        out_specs=pl.BlockSpec((tl, W), row),
        out_shape=jax.ShapeDtypeStruct((T, W), BF16),
        compiler_params=_params("parallel"), name="ssd_out",
    )(y, xbc, p_main, jnp.repeat(d_skip, SSD_HEAD_DIM).reshape(1, W), norm_w.reshape(1, W))


def _pool_kernel(prev_ref, v_ref, next_ref, band_ref, pw_ref, pb_ref, ps_ref, o_ref, *, rows):
    i = pl.program_id(1)
    nl = pl.num_programs(1)
    tl = v_ref.shape[0]
    gw = v_ref.shape[1] // len(POOL_WINDOWS)
    tile = band_ref.shape[1]
    pos = lax.broadcasted_iota(jnp.int32, (tl, 128), 0)
    r = i * (tl // GRID_W) + lax.shift_right_logical(pos, GRID_W.bit_length() - 1)
    cidx = lax.bitwise_and(pos, GRID_W - 1)
    for g, w in enumerate(POOL_WINDOWS):
        lo, hi = -(w // 2), w - 1 - w // 2
        cols = slice(g * gw, (g + 1) * gw)
        pv = jnp.where(i > 0, prev_ref[:, cols], jnp.zeros((), BF16))
        nx = jnp.where(i < nl - 1, next_ref[:, cols], jnp.zeros((), BF16))
        vg = v_ref[:, cols]
        ext = jnp.concatenate([pv, vg, nx], axis=0)
        band = band_ref[g]
        cs = jnp.concatenate(
            [jnp.dot(band, ext[t * tile:(t + 1) * tile], preferred_element_type=F32)
             for t in range(ext.shape[0] // tile)], axis=0)
        rs = None
        for o in range(lo, hi + 1):
            s0 = POOL_HALO + GRID_W * o
            term = cs[s0:s0 + tl]
            rs = term if rs is None else rs + term
        n_r = jnp.minimum(r + hi, rows - 1) - jnp.maximum(r + lo, 0) + 1
        n_c = jnp.minimum(cidx + hi, GRID_W - 1) - jnp.maximum(cidx + lo, 0) + 1
        inv = 1.0 / (n_r * n_c).astype(F32)
        inv = jnp.concatenate([inv] * (gw // 128), axis=1)
        d = rs * inv - vg.astype(F32)
        yv = jnp.dot(d.astype(BF16), pw_ref[g], preferred_element_type=F32) + pb_ref[:, cols]
        o_ref[:, cols] = (yv * ps_ref[:, cols]).astype(o_ref.dtype)


def _pool_bands():
    t = 2 * GRID_W
    i = np.arange(t)[:, None]
    j = np.arange(t)[None, :]
    out = []
    for w in POOL_WINDOWS:
        lo, hi = -(w // 2), w - 1 - w // 2
        dc = j % GRID_W - i % GRID_W
        out.append(((i // GRID_W == j // GRID_W) & (dc >= lo) & (dc <= hi)).astype(np.float32))
    return jnp.asarray(np.stack(out), dtype=BF16)


def _pool_mixer(p_main, pool_w, pool_b, pool_scale, Bsz, L, *, tl=1024):
    T = p_main.shape[0]
    W = p_main.shape[1] // 4
    tl = min(tl, L)
    assert L % tl == 0 and tl % POOL_HALO == 0
    nl = L // tl
    lh = L // POOL_HALO
    th = tl // POOL_HALO
    cb = 3
    full = lambda shape: pl.BlockSpec(shape, lambda b, i: (0,) * len(shape))
    return pl.pallas_call(
        functools.partial(_pool_kernel, rows=L // GRID_W), grid=(Bsz, nl),
        in_specs=[
            pl.BlockSpec((POOL_HALO, W), lambda b, i: (jnp.maximum(b * lh + i * th - 1, b * lh), cb)),
            pl.BlockSpec((tl, W), lambda b, i: (b * nl + i, cb)),
            pl.BlockSpec((POOL_HALO, W), lambda b, i: (jnp.minimum(b * lh + (i + 1) * th, (b + 1) * lh - 1), cb)),
            full((len(POOL_WINDOWS), 2 * GRID_W, 2 * GRID_W)),
            full(pool_w.shape), full((1, W)), full((1, W)),
        ],
        out_specs=pl.BlockSpec((tl, W), lambda b, i: (b * nl + i, 0)),
        out_shape=jax.ShapeDtypeStruct((T, W), BF16),
        compiler_params=_params("parallel", "parallel"), name="pool_mixer",
    )(p_main, p_main, p_main, _pool_bands(), pool_w.astype(BF16),
      pool_b.reshape(1, W), pool_scale.reshape(1, W))


def _gmlp_kernel(u_ref, v_ref, g_ref, b_ref, ws_ref, bs_ref, o_ref):
    v = v_ref[...].astype(F32)
    mu = jnp.mean(v, axis=-1, keepdims=True)
    vc = v - mu
    var = jnp.mean(vc * vc, axis=-1, keepdims=True)
    vn = (vc * lax.rsqrt(var + EPS) * g_ref[...] + b_ref[...]).astype(BF16)
    tl, W = v.shape
    hd = W // CG_HEADS
    for n in range(tl // CG_CHUNK):
        rs = slice(n * CG_CHUNK, (n + 1) * CG_CHUNK)
        for h in range(CG_HEADS):
            cs = slice(h * hd, (h + 1) * hd)
            sv = jnp.dot(ws_ref[h], vn[rs, cs], preferred_element_type=F32) + bs_ref[:, h:h + 1]
            o_ref[rs, cs] = (u_ref[rs, cs].astype(F32) * sv).astype(o_ref.dtype)


def _gmlp_gate(uv, ln_g, ln_b, ws, bs, *, tl=256):
    T = uv.shape[0]
    W = uv.shape[1] // 2
    tl = min(tl, T)
    one = pl.BlockSpec((1, W), lambda i: (0, 0))
    return pl.pallas_call(
        _gmlp_kernel, grid=(T // tl,),
        in_specs=[pl.BlockSpec((tl, W), lambda i: (i, 0)), pl.BlockSpec((tl, W), lambda i: (i, 1)),
                  one, one,
                  pl.BlockSpec(ws.shape, lambda i: (0, 0, 0)),
                  pl.BlockSpec((CG_CHUNK, CG_HEADS), lambda i: (0, 0))],
        out_specs=pl.BlockSpec((tl, W), lambda i: (i, 0)),
        out_shape=jax.ShapeDtypeStruct((T, W), BF16),
        compiler_params=_params("parallel"), name="gmlp_gate",
    )(uv, uv, ln_g.reshape(1, W), ln_b.reshape(1, W), ws.astype(BF16), jnp.transpose(bs))


def _start_rows(n, per_iter, make_copy):
    assert n % per_iter == 0

    def issue(it, carry):
        for u in range(per_iter):
            make_copy(it, u).start(priority=u % 2)
        return carry

    lax.fori_loop(0, n // per_iter, issue, 0)


def _wait_rows(n, per_iter, make_copy):
    def drain(it, carry):
        for u in range(per_iter):
            make_copy(it, u).wait()
        return carry

    lax.fori_loop(0, n // per_iter, drain, 0)


def _gather_kernel(nu_ref, tok_ref, nxt_ref, f_hbm, o_ref, buf, sem):
    i = pl.program_id(0)
    tb, D = o_ref.shape
    nch = D // LANES
    slot = i % 2

    def row_copy(idx_ref, s):
        def make(it, u):
            src = pl.multiple_of(idx_ref[0, 0, it * F32_SUBLANES + u], nch)
            dst = pl.multiple_of(it * (F32_SUBLANES * nch), F32_SUBLANES * nch) + u * nch
            return pltpu.make_async_copy(f_hbm.at[pl.ds(src, nch), :], buf.at[s, pl.ds(dst, nch), :], sem.at[s])
        return make

    @pl.when(jnp.logical_and(i == 0, nu_ref[0] > 0))
    def _():
        _start_rows(tb, F32_SUBLANES, row_copy(tok_ref, 0))

    @pl.when(i + 1 < nu_ref[0])
    def _():
        _start_rows(tb, F32_SUBLANES, row_copy(nxt_ref, 1 - slot))

    @pl.when(i < nu_ref[0])
    def _():
        _wait_rows(tb, F32_SUBLANES, row_copy(tok_ref, slot))
        cur = buf.at[slot]
        xv = jnp.concatenate([cur[pl.ds(ch, tb, stride=nch), :] for ch in range(nch)], axis=1)
        o_ref[...] = xv.astype(o_ref.dtype)

    @pl.when(i >= nu_ref[0])
    def _():
        o_ref[...] = jnp.zeros(o_ref.shape, o_ref.dtype)


def _moe_gather(f_tok, row_tok, n_used, D):
    nch = D // LANES
    P = row_tok.shape[0]
    nb = P // MOE_TM
    tok3 = (row_tok * nch).reshape(nb, 1, MOE_TM)
    grid_spec = pltpu.PrefetchScalarGridSpec(
        num_scalar_prefetch=1, grid=(nb,),
        in_specs=[pl.BlockSpec((1, 1, MOE_TM), lambda i, nu: (i, 0, 0), memory_space=pltpu.SMEM),
                  pl.BlockSpec((1, 1, MOE_TM), lambda i, nu: (jnp.minimum(i + 1, nb - 1), 0, 0),
                               memory_space=pltpu.SMEM),
                  pl.BlockSpec(memory_space=pl.ANY)],
        out_specs=pl.BlockSpec((MOE_TM, D), lambda i, nu: (i, 0)),
        scratch_shapes=[pltpu.VMEM((2, MOE_TM * nch, LANES), F32), pltpu.SemaphoreType.DMA((2,))])
    return pl.pallas_call(
        _gather_kernel, grid_spec=grid_spec,
        out_shape=jax.ShapeDtypeStruct((P, D), BF16),
        compiler_params=_params("arbitrary"), name="moe_gather",
    )(n_used, tok3, tok3, f_tok)


def _expert_changed(be_ref):
    i = pl.program_id(1)
    return jnp.logical_or(i == 0, be_ref[i] != be_ref[jnp.maximum(i - 1, 0)])


def _gu_kernel(be_ref, nu_ref, x_ref, wg_ref, wu_ref, bg_ref, bu_ref, o_ref, wg_bf, wu_bf):
    @pl.when(pl.program_id(1) < nu_ref[0])
    def _():
        @pl.when(_expert_changed(be_ref))
        def _():
            wg_bf[...] = wg_ref[0].astype(BF16)
            wu_bf[...] = wu_ref[0].astype(BF16)

        xv = x_ref[...]
        gv = jnp.dot(xv, wg_bf[...], preferred_element_type=F32) + bg_ref[0]
        uv = jnp.dot(xv, wu_bf[...], preferred_element_type=F32) + bu_ref[0]
        gv = jnp.minimum(gv, SWIGLU_LIMIT)
        uv = jnp.clip(uv, -SWIGLU_LIMIT, SWIGLU_LIMIT)
        act = gv * (1.0 / (1.0 + jnp.exp(-SWIGLU_ALPHA * gv))) * (uv + 1.0)
        o_ref[...] = act.astype(o_ref.dtype)

    @pl.when(pl.program_id(1) >= nu_ref[0])
    def _():
        o_ref[...] = jnp.zeros(o_ref.shape, o_ref.dtype)


def _moe_gate_up(xs, block_e, n_used, w_gu, b_gu, *, tn=1024):
    P, D = xs.shape
    F = w_gu.shape[2] // 2
    nj = F // tn
    nb = P // MOE_TM
    b3 = b_gu.reshape(-1, 1, 2 * F)
    grid_spec = pltpu.PrefetchScalarGridSpec(
        num_scalar_prefetch=2, grid=(nj, nb),
        in_specs=[pl.BlockSpec((MOE_TM, D), lambda j, i, be, nu: (i, 0)),
                  pl.BlockSpec((1, D, tn), lambda j, i, be, nu: (be[i], 0, j)),
                  pl.BlockSpec((1, D, tn), lambda j, i, be, nu: (be[i], 0, nj + j)),
                  pl.BlockSpec((1, 1, tn), lambda j, i, be, nu: (be[i], 0, j)),
                  pl.BlockSpec((1, 1, tn), lambda j, i, be, nu: (be[i], 0, nj + j))],
        out_specs=pl.BlockSpec((MOE_TM, tn), lambda j, i, be, nu: (i, j)),
        scratch_shapes=[pltpu.VMEM((D, tn), BF16), pltpu.VMEM((D, tn), BF16)])
    return pl.pallas_call(
        _gu_kernel, grid_spec=grid_spec,
        out_shape=jax.ShapeDtypeStruct((P, F), BF16),
        compiler_params=_params("arbitrary", "arbitrary"), name="moe_gate_up",
    )(block_e, n_used, xs, w_gu, w_gu, b3, b3)


def _down_kernel(be_ref, nu_ref, a_ref, w_ref, b_ref, o_ref, w_bf):
    @pl.when(pl.program_id(1) < nu_ref[0])
    def _():
        @pl.when(_expert_changed(be_ref))
        def _():
            w_bf[...] = w_ref[0].astype(BF16)

        o_ref[...] = jnp.dot(a_ref[...], w_bf[...], preferred_element_type=F32) + b_ref[0]

    @pl.when(pl.program_id(1) >= nu_ref[0])
    def _():
        o_ref[...] = jnp.zeros(o_ref.shape, o_ref.dtype)


def _moe_down(act, block_e, n_used, w_down, b_down, *, tn=1024):
    P, F = act.shape
    D = w_down.shape[2]
    nb = P // MOE_TM
    grid_spec = pltpu.PrefetchScalarGridSpec(
        num_scalar_prefetch=2, grid=(D // tn, nb),
        in_specs=[pl.BlockSpec((MOE_TM, F), lambda j, i, be, nu: (i, 0)),
                  pl.BlockSpec((1, F, tn), lambda j, i, be, nu: (be[i], 0, j)),
                  pl.BlockSpec((1, 1, tn), lambda j, i, be, nu: (be[i], 0, j))],
        out_specs=pl.BlockSpec((MOE_TM, tn), lambda j, i, be, nu: (i, j)),
        scratch_shapes=[pltpu.VMEM((F, tn), BF16)])
    return pl.pallas_call(
        _down_kernel, grid_spec=grid_spec,
        out_shape=jax.ShapeDtypeStruct((P, D), F32),
        compiler_params=_params("arbitrary", "arbitrary"), name="moe_down",
    )(block_e, n_used, act, w_down, b_down.reshape(-1, 1, D))


def _combine_kernel(dest_ref, nxt_ref, gt_ref, x_ref, g_ref, sc_ref, sh_ref, ys_hbm, *rest, want_x):
    if want_x:
        xo_ref, h_ref, buf, sem = rest
    else:
        h_ref, buf, sem = rest
    i = pl.program_id(0)
    tb, D = x_ref.shape
    per_iter = F32_SUBLANES * TOP_K
    slot = i % 2

    def row_copy(idx_ref, s):
        def make(it, u):
            return pltpu.make_async_copy(ys_hbm.at[pl.ds(idx_ref[0, 0, it * per_iter + u], 1), :],
                                         buf.at[s, u % TOP_K, it, pl.ds(u // TOP_K, 1), :], sem.at[s])
        return make

    @pl.when(i == 0)
    def _():
        _start_rows(tb * TOP_K, per_iter, row_copy(dest_ref, 0))

    @pl.when(i + 1 < pl.num_programs(0))
    def _():
        _start_rows(tb * TOP_K, per_iter, row_copy(nxt_ref, 1 - slot))

    _wait_rows(tb * TOP_K, per_iter, row_copy(dest_ref, slot))
    moe = None
    for k in range(TOP_K):
        term = buf[slot, k].reshape(tb, D) * gt_ref[:, k:k + 1]
        moe = term if moe is None else moe + term
    xv = x_ref[...] + g_ref[0] * moe
    if want_x:
        xo_ref[...] = xv
    h_ref[...] = _rms_mod(xv, sc_ref[0], sh_ref[0]).astype(h_ref.dtype)


def _moe_combine(ys, dest, gates, x2, g, sc, sh, L, *, want_x, out_dtype, tb=256):
    T, D = x2.shape
    tb = min(tb, L)
    nl = L // tb
    mod = pl.BlockSpec((1, 1, D), lambda i: (i // nl, 0, 0))
    row = pl.BlockSpec((tb, D), lambda i: (i, 0))
    out_specs = [row, row] if want_x else [row]
    out_shape = ([jax.ShapeDtypeStruct((T, D), F32)] if want_x else []) + [jax.ShapeDtypeStruct((T, D), out_dtype)]
    nb = T // tb
    dest3 = dest.reshape(nb, 1, tb * TOP_K)
    return pl.pallas_call(
        functools.partial(_combine_kernel, want_x=want_x), grid=(nb,),
        in_specs=[pl.BlockSpec((1, 1, tb * TOP_K), lambda i: (i, 0, 0), memory_space=pltpu.SMEM),
                  pl.BlockSpec((1, 1, tb * TOP_K), lambda i: (jnp.minimum(i + 1, nb - 1), 0, 0),
                               memory_space=pltpu.SMEM),
                  pl.BlockSpec((tb, TOP_K), lambda i: (i, 0)),
                  row, mod, mod, mod, pl.BlockSpec(memory_space=pl.ANY)],
        out_specs=out_specs, out_shape=out_shape,
        scratch_shapes=[pltpu.VMEM((2, TOP_K, tb // F32_SUBLANES, F32_SUBLANES, D), F32),
                        pltpu.SemaphoreType.DMA((2,))],
        compiler_params=_params("arbitrary"), name="moe_combine",
    )(dest3, dest3, gates, x2, g, sc, sh, ys)


def _moe_routing(top_i):
    T = top_i.shape[0]
    A = T * TOP_K
    P = A + N_EXPERTS * MOE_TM
    nb = P // MOE_TM
    flat_e = top_i.reshape(A)
    onehot = (flat_e[:, None] == jnp.arange(N_EXPERTS, dtype=jnp.int32)[None, :]).astype(jnp.int32)
    cs = jnp.cumsum(onehot, axis=0)
    rank = jnp.take_along_axis(cs, flat_e[:, None], axis=1)[:, 0] - 1
    counts = cs[-1]
    padded = (counts + MOE_TM - 1) // MOE_TM * MOE_TM
    pad_end = jnp.cumsum(padded)
    pad_start = pad_end - padded
    dest = (pad_start[flat_e] + rank).astype(jnp.int32)
    block_e = jnp.minimum(jnp.searchsorted(pad_end, jnp.arange(nb, dtype=jnp.int32) * MOE_TM, side="right"),
                          N_EXPERTS - 1).astype(jnp.int32)
    n_used = (pad_end[-1:] // MOE_TM).astype(jnp.int32)
    _, sorted_tok = lax.sort((dest, jnp.arange(A, dtype=jnp.int32) // TOP_K), num_keys=1)
    start = jnp.cumsum(counts) - counts
    row = jnp.arange(P, dtype=jnp.int32)
    row_e = jnp.repeat(block_e, MOE_TM)
    off = row - pad_start[row_e]
    src = jnp.clip(start[row_e] + off, 0, A - 1)
    row_tok = jnp.where(off < counts[row_e], sorted_tok[src], 0).astype(jnp.int32)
    return dest, row_tok, block_e, n_used


def _moe(x2, m, g1, sc2, sh2, g2, nsc, nsh, router_w, router_b, w_gu, b_gu, w_down, b_down, L,
         *, layer, want_x, out_dtype):
    xo, f, top_i, gates = _res_norm_router(x2, m, g1, sc2, sh2, jnp.transpose(router_w),
                                           router_b.reshape(N_EXPERTS, 1), L)
    dest, row_tok, block_e, n_used = _moe_routing(top_i)
    xs = _moe_gather(f, row_tok, n_used, x2.shape[1])
    block_w = block_e + layer * N_EXPERTS
    act = _moe_gate_up(xs, block_w, n_used, w_gu.reshape((-1,) + w_gu.shape[2:]), b_gu)
    ys = _moe_down(act, block_w, n_used, w_down.reshape((-1,) + w_down.shape[2:]), b_down)
    return _moe_combine(ys, dest, gates, xo, g2, nsc, nsh, L, want_x=want_x, out_dtype=out_dtype)


def _ssd_tri():
    l = np.arange(SSD_CHUNK)[:, None]
    j = np.arange(SSD_CHUNK)[None, :]
    return jnp.asarray(np.stack([(j <= l), (j >= l)]).astype(np.float32))


def _ssd_stream(h_bf, w_main, w_dt, conv_w, conv_b, a_log, dt_bias, h0, Bsz, L, width):
    T = h_bf.shape[0]
    H = width // SSD_HEAD_DIM
    p_main = _matmul([(h_bf, w_main)], tm=1024, tn=1024, out_dtype=BF16, name="even_in_proj")
    dt_raw = _matmul([(h_bf, w_dt)], tm=1024, tn=w_dt.shape[1], out_dtype=F32, fold=3, name="dt_proj")
    dt2 = jnp.transpose(dt_raw[:, :2 * H].reshape(T, 2, H), (1, 0, 2))
    xbc = _ssd_conv(p_main, conv_w, conv_b, Bsz, L)
    y, h_last = _ssd_scan(xbc, dt2, _ssd_tri(), a_log, dt_bias, h0, Bsz, L, width)
    return p_main, xbc, y, h_last


def kernel(x, c, ctx, c_ctx, ada_w, ada_b, even_in_w, ssd_conv_w, ssd_conv_b, ssd_a_log, ssd_dt_bias,
           ssd_d, ssd_norm_w, pool_w, pool_b, pool_scale, even_out_w, cg_in_w, cg_ln_g, cg_ln_b, cg_ws,
           cg_bs, cg_out_w, router_w, router_b, moe_w_gu, moe_b_gu, moe_w_down, moe_b_down, final_norm_w):
    Bsz, S, D = x.shape
    Lc = ctx.shape[1]
    T = Bsz * S
    width = D
    H = width // SSD_HEAD_DIM
    gn = SSD_GROUPS * SSD_STATE
    x2 = x.reshape(T, D)
    ctx2 = ctx.reshape(Bsz * Lc, D)

    cin = jnp.zeros((8, D), F32).at[:Bsz].set(c).at[Bsz].set(c_ctx)
    mods = [_matmul([(cin, ada_w[i])], tm=8, tn=1024, out_dtype=F32, bias=ada_b[i], pre_act="silu",
                    precision=HIGHEST, name="ada_mod") for i in range(DEPTH)]

    def lat_mods(i):
        return [mods[i][:Bsz, k * D:(k + 1) * D].reshape(Bsz, 1, D) for k in range(6)]

    sh1, sc1, g1, sh2, sc2, g2 = lat_mods(0)
    sh_c = jnp.broadcast_to(mods[0][Bsz, :D].reshape(1, 1, D), (Bsz, 1, D))
    sc_c = jnp.broadcast_to(mods[0][Bsz, D:2 * D].reshape(1, 1, D), (Bsz, 1, D))
    w_in = even_in_w[0]
    o_dt = 2 * width + 2 * gn
    w_main = jnp.concatenate([w_in[:, :o_dt], w_in[:, o_dt + 2 * H:]], axis=1).astype(BF16)
    w_dt = _split3(jnp.pad(w_in[:, o_dt:o_dt + 2 * H], ((0, 0), (0, 128 - 2 * H))))
    ssd_p = (w_main, w_dt, ssd_conv_w[0], ssd_conv_b[0], ssd_a_log[0], ssd_dt_bias[0])

    h_c = _norm_mod(ctx2, sc_c, sh_c, Lc)
    h0 = jnp.zeros((2, Bsz, SSD_GROUPS, SSD_STATE, width // SSD_GROUPS), F32)
    _, _, _, ctx_state = _ssd_stream(h_c, *ssd_p, h0, Bsz, Lc, width)

    h_l = _norm_mod(x2, sc1, sh1, S)
    p_main, xbc, y, _ = _ssd_stream(h_l, *ssd_p, ctx_state, Bsz, S, width)
    y_ssd = _ssd_out(y, xbc, p_main, ssd_d[0], ssd_norm_w[0])
    y_pool = _pool_mixer(p_main, pool_w[0], pool_b[0], pool_scale[0], Bsz, S)
    w_out = even_out_w[0].astype(BF16)
    m = _matmul([(y_ssd, w_out[:width]), (y_pool, w_out[width:])], tm=1024, tn=1024, out_dtype=F32,
                name="even_out_proj")

    nsh1, nsc1, ng1, nsh2, nsc2, ng2 = lat_mods(1)
    x2, h_l = _moe(x2, m, g1, sc2, sh2, g2, nsc1, nsh1, router_w[0], router_b[0], moe_w_gu, moe_b_gu,
                   moe_w_down, moe_b_down, S, layer=0, want_x=True, out_dtype=BF16)

    uv = _matmul([(h_l, cg_in_w[0].astype(BF16))], tm=1024, tn=1024, out_dtype=BF16, act="gelu",
                 name="cg_in_proj")
    gated = _gmlp_gate(uv, cg_ln_g[0], cg_ln_b[0], cg_ws[0], cg_bs[0])
    m = _matmul([(gated, cg_out_w[0].astype(BF16))], tm=1024, tn=1024, out_dtype=F32, name="cg_out_proj")
    fin_sc = jnp.broadcast_to((final_norm_w - 1.0).reshape(1, 1, D), (Bsz, 1, D))
    fin_sh = jnp.zeros((Bsz, 1, D), F32)
    (out,) = _moe(x2, m, ng1, nsc2, nsh2, ng2, fin_sc, fin_sh, router_w[1], router_b[1], moe_w_gu,
                  moe_b_gu, moe_w_down, moe_b_down, S, layer=1, want_x=False, out_dtype=F32)
    return out.reshape(Bsz, S, D)
```

```python
import functools
import math

import numpy as np
import jax
import jax.numpy as jnp
from jax import lax
from jax.experimental import pallas as pl
from jax.experimental.pallas import tpu as pltpu

F32 = jnp.float32
BF16 = jnp.bfloat16
HIGHEST = lax.Precision.HIGHEST

GRID_W = 64
EPS = 1e-6
DEPTH = 2

SSD_HEAD_DIM = 64
SSD_GROUPS = 8
SSD_STATE = 128
SSD_CONV = 4
SSD_CHUNK = 128

POOL_WINDOWS = (2, 4, 8, 16)
POOL_HALO = 512

CG_HEADS = 8
CG_CHUNK = 128

N_EXPERTS = 32
TOP_K = 4
SWIGLU_ALPHA = 1.702
SWIGLU_LIMIT = 7.0
MOE_TM = 512
F32_SUBLANES = 8

V7X_VMEM_LIMIT_BYTES = 56 * 1024 * 1024
BF16_SUBLANES = 16


def _params(*sem):
    return pltpu.CompilerParams(dimension_semantics=sem, vmem_limit_bytes=V7X_VMEM_LIMIT_BYTES)


def _mm_kernel(*refs, n_pairs, has_bias, pre_act, act, precision, fold):
    o_ref = refs[-1]
    acc = None
    for p in range(n_pairs):
        xv = refs[2 * p][...]
        wv = refs[2 * p + 1][...]
        if precision is not None:
            xv = xv.astype(F32)
        if pre_act == "silu":
            xv = xv * (1.0 / (1.0 + jnp.exp(-xv)))
        d = jnp.dot(xv, wv, preferred_element_type=F32, precision=precision)
        acc = d if acc is None else acc + d
    if has_bias:
        acc = acc + refs[2 * n_pairs][...]
    if act == "gelu":
        acc = 0.5 * acc * (1.0 + lax.erf(acc * (1.0 / math.sqrt(2.0))))
    if fold > 1:
        wn = acc.shape[1] // fold
        acc = sum(acc[:, k * wn:(k + 1) * wn] for k in range(fold))
    o_ref[...] = acc.astype(o_ref.dtype)


def _matmul(pairs, *, tm, tn, out_dtype, name, bias=None, pre_act=None, act=None, precision=None, fold=1):
    M = pairs[0][0].shape[0]
    N = pairs[0][1].shape[1]
    tm = min(tm, M)
    tn = min(tn, N)
    assert M % tm == 0 and N % tn == 0 and (fold == 1 or tn == N)
    in_specs, args = [], []
    for xa, wa in pairs:
        K = xa.shape[1]
        in_specs += [pl.BlockSpec((tm, K), lambda i, j: (i, 0)),
                     pl.BlockSpec((K, tn), lambda i, j: (0, j))]
        args += [xa, wa]
    if bias is not None:
        in_specs.append(pl.BlockSpec((1, tn), lambda i, j: (0, j)))
        args.append(bias.reshape(1, N))
    kern = functools.partial(_mm_kernel, n_pairs=len(pairs), has_bias=bias is not None,
                             pre_act=pre_act, act=act, precision=precision, fold=fold)
    return pl.pallas_call(
        kern, grid=(M // tm, N // tn), in_specs=in_specs,
        out_specs=pl.BlockSpec((tm, tn // fold), lambda i, j: (i, j)),
        out_shape=jax.ShapeDtypeStruct((M, N // fold), out_dtype),
        compiler_params=_params("parallel", "parallel"), name=name)(*args)


def _rms_mod(xv, sc, sh):
    ms = jnp.mean(xv * xv, axis=-1, keepdims=True)
    return xv * lax.rsqrt(ms + EPS) * (1.0 + sc) + sh


def _norm_kernel(x_ref, sc_ref, sh_ref, h_ref):
    h_ref[...] = _rms_mod(x_ref[...], sc_ref[0], sh_ref[0]).astype(h_ref.dtype)


def _norm_mod(x2, sc, sh, L, *, tl=512):
    T, D = x2.shape
    tl = min(tl, L)
    nl = L // tl
    mod = pl.BlockSpec((1, 1, D), lambda i: (i // nl, 0, 0))
    return pl.pallas_call(
        _norm_kernel, grid=(T // tl,),
        in_specs=[pl.BlockSpec((tl, D), lambda i: (i, 0)), mod, mod],
        out_specs=pl.BlockSpec((tl, D), lambda i: (i, 0)),
        out_shape=jax.ShapeDtypeStruct((T, D), BF16),
        compiler_params=_params("parallel"), name="norm_mod")(x2, sc, sh)


def _res_router_kernel(x_ref, m_ref, g_ref, sc_ref, sh_ref, rw_ref, rb_ref,
                       xo_ref, f_ref, ti_ref, tg_ref):
    xv = x_ref[...] + g_ref[0] * m_ref[...]
    xo_ref[...] = xv
    f = _rms_mod(xv, sc_ref[0], sh_ref[0])
    f_ref[...] = f
    lg = lax.dot_general(rw_ref[...], f, (((1,), (1,)), ((), ())),
                         precision=HIGHEST, preferred_element_type=F32) + rb_ref[...]
    iota = lax.broadcasted_iota(jnp.int32, lg.shape, 0)
    vals, idxs = [], []
    for _ in range(TOP_K):
        mx = jnp.max(lg, axis=0, keepdims=True)
        ix = jnp.min(jnp.where(lg == mx, iota, N_EXPERTS), axis=0, keepdims=True)
        vals.append(mx)
        idxs.append(ix)
        lg = jnp.where(iota == ix, -jnp.inf, lg)
    tv = jnp.concatenate(vals, axis=0)
    ex = jnp.exp(tv - vals[0])
    ti_ref[0] = jnp.concatenate(idxs, axis=0)
    tg_ref[0] = ex / jnp.sum(ex, axis=0, keepdims=True)


def _res_norm_router(x2, m, g, sc, sh, rw_t, rb, L, *, tl=512):
    T, D = x2.shape
    tl = min(tl, L)
    nl = L // tl
    nb = T // tl
    mod = pl.BlockSpec((1, 1, D), lambda i: (i // nl, 0, 0))
    row = pl.BlockSpec((tl, D), lambda i: (i, 0))
    tk = pl.BlockSpec((1, TOP_K, tl), lambda i: (i, 0, 0))
    xo, f, ti, tg = pl.pallas_call(
        _res_router_kernel, grid=(nb,),
        in_specs=[row, row, mod, mod, mod,
                  pl.BlockSpec((N_EXPERTS, D), lambda i: (0, 0)),
                  pl.BlockSpec((N_EXPERTS, 1), lambda i: (0, 0))],
        out_specs=[row, row, tk, tk],
        out_shape=[jax.ShapeDtypeStruct((T, D), F32), jax.ShapeDtypeStruct((T, D), F32),
                   jax.ShapeDtypeStruct((nb, TOP_K, tl), jnp.int32),
                   jax.ShapeDtypeStruct((nb, TOP_K, tl), F32)],
        compiler_params=_params("parallel"), name="res_norm_router")(x2, m, g, sc, sh, rw_t, rb)
    top_i = jnp.transpose(ti, (0, 2, 1)).reshape(T, TOP_K)
    gates = jnp.transpose(tg, (0, 2, 1)).reshape(T, TOP_K)
    return xo, f, top_i, gates


def _conv_kernel(prev_ref, main_ref, next_ref, w_ref, b_ref, o_ref):
    i = pl.program_id(1)
    nl = pl.num_programs(1)
    tl = main_ref.shape[0]
    hr = prev_ref.shape[0]
    pv = jnp.where(i > 0, prev_ref[...].astype(F32), 0.0)
    nx = jnp.where(i < nl - 1, next_ref[...].astype(F32), 0.0)
    ext = jnp.concatenate([pv, main_ref[...].astype(F32), nx], axis=0)
    n = ext.shape[0]
    acc = None
    for k in range(SSD_CONV):
        off = k - SSD_CONV // 2
        sh = (-off) % n
        term = (ext if sh == 0 else pltpu.roll(ext, sh, 0))[hr:hr + tl] * w_ref[k:k + 1, :]
        acc = term if acc is None else acc + term
    acc = acc + b_ref[...]
    o_ref[...] = (acc * (1.0 / (1.0 + jnp.exp(-acc)))).astype(o_ref.dtype)


def _ssd_conv(p_main, conv_w, conv_b, Bsz, L, *, tl=512):
    T = p_main.shape[0]
    W = p_main.shape[1] // 4
    tl = min(tl, L)
    nl = L // tl
    hr = BF16_SUBLANES
    lh = L // hr
    th = tl // hr
    return pl.pallas_call(
        _conv_kernel, grid=(Bsz, nl, 2),
        in_specs=[
            pl.BlockSpec((hr, W), lambda b, i, c: (jnp.maximum(b * lh + i * th - 1, b * lh), 1 + c)),
            pl.BlockSpec((tl, W), lambda b, i, c: (b * nl + i, 1 + c)),
            pl.BlockSpec((hr, W), lambda b, i, c: (jnp.minimum(b * lh + (i + 1) * th, (b + 1) * lh - 1), 1 + c)),
            pl.BlockSpec((SSD_CONV, W), lambda b, i, c: (0, c)),
            pl.BlockSpec((1, W), lambda b, i, c: (0, c)),
        ],
        out_specs=pl.BlockSpec((tl, W), lambda b, i, c: (b * nl + i, c)),
        out_shape=jax.ShapeDtypeStruct((T, 2 * W), BF16),
        compiler_params=_params("parallel", "parallel", "parallel"), name="ssd_conv",
    )(p_main, p_main, p_main, conv_w, conv_b.reshape(1, -1))


def _split3(v):
    c1 = v.astype(BF16)
    r1 = v - c1.astype(F32)
    c2 = r1.astype(BF16)
    c3 = (r1 - c2.astype(F32)).astype(BF16)
    return jnp.concatenate([c1, c2, c3], axis=1)


def _scan_kernel(xbc_ref, dt_ref, tri_ref, alog_ref, dtb_ref, eh_ref, es_ref, h0_ref, y_ref, h_ref, *, width):
    c = pl.program_id(2)

    @pl.when(c == 0)
    def _():
        h_ref[...] = h0_ref[...]

    n_heads = width // SSD_HEAD_DIM
    hpg = n_heads // SSD_GROUPS
    gw = hpg * SSD_HEAD_DIM
    gn = SSD_GROUPS * SSD_STATE
    raw = dt_ref[0] + dtb_ref[0]
    dt = jnp.maximum(raw, 0.0) + jnp.log1p(jnp.exp(-jnp.abs(raw)))
    da = dt * (-jnp.exp(alog_ref[0]))
    tri = tri_ref[0]
    cum = jnp.dot(tri, da, precision=HIGHEST, preferred_element_type=F32)
    cum_t = lax.dot_general(da, tri, (((0,), (1,)), ((), ())),
                            precision=HIGHEST, preferred_element_type=F32)
    tot = jnp.broadcast_to(jnp.sum(da, axis=0, keepdims=True), (8, n_heads))
    cum3 = _split3(cum)
    cum_hd = jnp.dot(cum3, eh_ref[...], preferred_element_type=F32)
    dt_hd = jnp.dot(_split3(dt), eh_ref[...], preferred_element_type=F32)
    tot_hd = jnp.dot(_split3(tot), eh_ref[...], preferred_element_type=F32)[0:1]
    cum_st = jnp.dot(cum3, es_ref[...], preferred_element_type=F32)
    e_cum = jnp.exp(cum_hd)
    e_tot = jnp.exp(tot_hd)
    xdt = xbc_ref[:, :width].astype(F32) * dt_hd
    xdt_b = xdt.astype(BF16)
    xdt_end = (xdt * jnp.exp(tot_hd - cum_hd)).astype(BF16)
    mask = tri > 0.0
    first_head = lax.broadcasted_iota(jnp.int32, (SSD_CHUNK, 2 * SSD_HEAD_DIM), 1) < SSD_HEAD_DIM
    zero = jnp.zeros((), BF16)
    for g in range(SSD_GROUPS):
        bg = xbc_ref[:, width + g * SSD_STATE: width + (g + 1) * SSD_STATE]
        cg = xbc_ref[:, width + gn + g * SSD_STATE: width + gn + (g + 1) * SSD_STATE]
        gm = lax.dot_general(cg, bg, (((1,), (1,)), ((), ())), preferred_element_type=F32)
        h_in = h_ref[0, 0, g]
        y_off = jnp.dot(cg, h_in.astype(BF16), preferred_element_type=F32)
        for pair in range(hpg // 2):
            scores = []
            for hh in (2 * pair, 2 * pair + 1):
                h = g * hpg + hh
                seg = cum_st[:, h * SSD_CHUNK:(h + 1) * SSD_CHUNK] - cum_t[h:h + 1, :]
                scores.append((gm * jnp.exp(jnp.where(mask, seg, -jnp.inf))).astype(BF16))
            c0 = g * gw + pair * 2 * SSD_HEAD_DIM
            xp = xdt_b[:, c0:c0 + 2 * SSD_HEAD_DIM]
            rhs = jnp.concatenate([jnp.where(first_head, xp, zero), jnp.where(first_head, zero, xp)], axis=0)
            yp = jnp.dot(jnp.concatenate(scores, axis=1), rhs, preferred_element_type=F32)
            yp = yp + y_off[:, pair * 2 * SSD_HEAD_DIM:(pair + 1) * 2 * SSD_HEAD_DIM] * e_cum[:, c0:c0 + 2 * SSD_HEAD_DIM]
            y_ref[0, :, c0:c0 + 2 * SSD_HEAD_DIM] = yp.astype(y_ref.dtype)
        st = lax.dot_general(bg, xdt_end[:, g * gw:(g + 1) * gw], (((0,), (0,)), ((), ())),
                             preferred_element_type=F32)
        h_ref[0, 0, g] = h_in * e_tot[:, g * gw:(g + 1) * gw] + st


def _head_spread(n_heads, per_head):
    e = np.kron(np.eye(n_heads, dtype=np.float32), np.ones((1, per_head), np.float32))
    return jnp.asarray(np.concatenate([e, e, e], axis=0), dtype=BF16)


def _ssd_scan(xbc, dt2, tri, a_log, dt_bias, h0, Bsz, L, width):
    T = xbc.shape[0]
    H = width // SSD_HEAD_DIM
    nc = L // SSD_CHUNK

    def rows(d, b, c):
        return b * nc + c + d * (nc - 1 - 2 * c)

    st = pl.BlockSpec((1, 1) + h0.shape[2:], lambda d, b, c: (d, b, 0, 0, 0))
    vec = pl.BlockSpec((1, 1, H), lambda d, b, c: (d, 0, 0))
    return pl.pallas_call(
        functools.partial(_scan_kernel, width=width), grid=(2, Bsz, nc),
        in_specs=[
            pl.BlockSpec((SSD_CHUNK, 2 * width), lambda d, b, c: (rows(d, b, c), 0)),
            pl.BlockSpec((1, SSD_CHUNK, H), lambda d, b, c: (d, rows(d, b, c), 0)),
            pl.BlockSpec((1, SSD_CHUNK, SSD_CHUNK), lambda d, b, c: (d, 0, 0)),
            vec, vec,
            pl.BlockSpec((3 * H, width), lambda d, b, c: (0, 0)),
            pl.BlockSpec((3 * H, H * SSD_CHUNK), lambda d, b, c: (0, 0)),
            st,
        ],
        out_specs=[pl.BlockSpec((1, SSD_CHUNK, width), lambda d, b, c: (d, rows(d, b, c), 0)), st],
        out_shape=[jax.ShapeDtypeStruct((2, T, width), BF16),
                   jax.ShapeDtypeStruct(h0.shape, F32)],
        compiler_params=_params("parallel", "parallel", "arbitrary"), name="ssd_scan",
    )(xbc, dt2, tri, a_log.reshape(2, 1, H), dt_bias.reshape(2, 1, H),
      _head_spread(H, SSD_HEAD_DIM), _head_spread(H, SSD_CHUNK), h0)


def _ssd_out_kernel(y_ref, x_ref, z_ref, d_ref, nw_ref, o_ref):
    xs = x_ref[...].astype(F32)
    z = z_ref[...].astype(F32)
    y = y_ref[0].astype(F32) + y_ref[1].astype(F32) + xs * d_ref[...]
    y = y * (z * (1.0 / (1.0 + jnp.exp(-z))))
    gw = y.shape[1] // SSD_GROUPS
    for g in range(SSD_GROUPS):
        seg = y[:, g * gw:(g + 1) * gw]
        ms = jnp.mean(seg * seg, axis=-1, keepdims=True)
        o_ref[:, g * gw:(g + 1) * gw] = (seg * lax.rsqrt(ms + EPS) * nw_ref[:, g * gw:(g + 1) * gw]
                                         ).astype(o_ref.dtype)


def _ssd_out(y, xbc, p_main, d_skip, norm_w, *, tl=512):
    T = xbc.shape[0]
    W = xbc.shape[1] // 2
    tl = min(tl, T)
    row = lambda i: (i, 0)
    one = pl.BlockSpec((1, W), lambda i: (0, 0))
    return pl.pallas_call(
        _ssd_out_kernel, grid=(T // tl,),
        in_specs=[pl.BlockSpec((2, tl, W), lambda i: (0, i, 0)),
                  pl.BlockSpec((tl, W), row), pl.BlockSpec((tl, W), row), one, one],
        out_specs=pl.BlockSpec((tl, W), row),
        out_shape=jax.ShapeDtypeStruct((T, W), BF16),
        compiler_params=_params("parallel"), name="ssd_out",
    )(y, xbc, p_main, jnp.repeat(d_skip, SSD_HEAD_DIM).reshape(1, W), norm_w.reshape(1, W))


def _pool_kernel(prev_ref, v_ref, next_ref, band_ref, pw_ref, pb_ref, ps_ref, o_ref, *, rows):
    i = pl.program_id(1)
    nl = pl.num_programs(1)
    tl = v_ref.shape[0]
    gw = v_ref.shape[1] // len(POOL_WINDOWS)
    tile = band_ref.shape[1]
    pos = lax.broadcasted_iota(jnp.int32, (tl, 128), 0)
    r = i * (tl // GRID_W) + lax.shift_right_logical(pos, GRID_W.bit_length() - 1)
    cidx = lax.bitwise_and(pos, GRID_W - 1)
    for g, w in enumerate(POOL_WINDOWS):
        lo, hi = -(w // 2), w - 1 - w // 2
        cols = slice(g * gw, (g + 1) * gw)
        pv = jnp.where(i > 0, prev_ref[:, cols], jnp.zeros((), BF16))
        nx = jnp.where(i < nl - 1, next_ref[:, cols], jnp.zeros((), BF16))
        vg = v_ref[:, cols]
        ext = jnp.concatenate([pv, vg, nx], axis=0)
        band = band_ref[g]
        cs = jnp.concatenate(
            [jnp.dot(band, ext[t * tile:(t + 1) * tile], preferred_element_type=F32)
             for t in range(ext.shape[0] // tile)], axis=0)
        rs = None
        for o in range(lo, hi + 1):
            s0 = POOL_HALO + GRID_W * o
            term = cs[s0:s0 + tl]
            rs = term if rs is None else rs + term
        n_r = jnp.minimum(r + hi, rows - 1) - jnp.maximum(r + lo, 0) + 1
        n_c = jnp.minimum(cidx + hi, GRID_W - 1) - jnp.maximum(cidx + lo, 0) + 1
        inv = 1.0 / (n_r * n_c).astype(F32)
        inv = jnp.concatenate([inv] * (gw // 128), axis=1)
        d = rs * inv - vg.astype(F32)
        yv = jnp.dot(d.astype(BF16), pw_ref[g], preferred_element_type=F32) + pb_ref[:, cols]
        o_ref[:, cols] = (yv * ps_ref[:, cols]).astype(o_ref.dtype)


def _pool_bands():
    t = 2 * GRID_W
    i = np.arange(t)[:, None]
    j = np.arange(t)[None, :]
    out = []
    for w in POOL_WINDOWS:
        lo, hi = -(w // 2), w - 1 - w // 2
        dc = j % GRID_W - i % GRID_W
        out.append(((i // GRID_W == j // GRID_W) & (dc >= lo) & (dc <= hi)).astype(np.float32))
    return jnp.asarray(np.stack(out), dtype=BF16)


def _pool_mixer(p_main, pool_w, pool_b, pool_scale, Bsz, L, *, tl=1024):
    T = p_main.shape[0]
    W = p_main.shape[1] // 4
    tl = min(tl, L)
    assert L % tl == 0 and tl % POOL_HALO == 0
    nl = L // tl
    lh = L // POOL_HALO
    th = tl // POOL_HALO
    cb = 3
    full = lambda shape: pl.BlockSpec(shape, lambda b, i: (0,) * len(shape))
    return pl.pallas_call(
        functools.partial(_pool_kernel, rows=L // GRID_W), grid=(Bsz, nl),
        in_specs=[
            pl.BlockSpec((POOL_HALO, W), lambda b, i: (jnp.maximum(b * lh + i * th - 1, b * lh), cb)),
            pl.BlockSpec((tl, W), lambda b, i: (b * nl + i, cb)),
            pl.BlockSpec((POOL_HALO, W), lambda b, i: (jnp.minimum(b * lh + (i + 1) * th, (b + 1) * lh - 1), cb)),
            full((len(POOL_WINDOWS), 2 * GRID_W, 2 * GRID_W)),
            full(pool_w.shape), full((1, W)), full((1, W)),
        ],
        out_specs=pl.BlockSpec((tl, W), lambda b, i: (b * nl + i, 0)),
        out_shape=jax.ShapeDtypeStruct((T, W), BF16),
        compiler_params=_params("parallel", "parallel"), name="pool_mixer",
    )(p_main, p_main, p_main, _pool_bands(), pool_w.astype(BF16),
      pool_b.reshape(1, W), pool_scale.reshape(1, W))


def _gmlp_kernel(u_ref, v_ref, g_ref, b_ref, ws_ref, bs_ref, o_ref):
    v = v_ref[...].astype(F32)
    mu = jnp.mean(v, axis=-1, keepdims=True)
    vc = v - mu
    var = jnp.mean(vc * vc, axis=-1, keepdims=True)
    vn = (vc * lax.rsqrt(var + EPS) * g_ref[...] + b_ref[...]).astype(BF16)
    tl, W = v.shape
    hd = W // CG_HEADS
    for n in range(tl // CG_CHUNK):
        rs = slice(n * CG_CHUNK, (n + 1) * CG_CHUNK)
        for h in range(CG_HEADS):
            cs = slice(h * hd, (h + 1) * hd)
            sv = jnp.dot(ws_ref[h], vn[rs, cs], preferred_element_type=F32) + bs_ref[:, h:h + 1]
            o_ref[rs, cs] = (u_ref[rs, cs].astype(F32) * sv).astype(o_ref.dtype)


def _gmlp_gate(uv, ln_g, ln_b, ws, bs, *, tl=256):
    T = uv.shape[0]
    W = uv.shape[1] // 2
    tl = min(tl, T)
    one = pl.BlockSpec((1, W), lambda i: (0, 0))
    return pl.pallas_call(
        _gmlp_kernel, grid=(T // tl,),
        in_specs=[pl.BlockSpec((tl, W), lambda i: (i, 0)), pl.BlockSpec((tl, W), lambda i: (i, 1)),
                  one, one,
                  pl.BlockSpec(ws.shape, lambda i: (0, 0, 0)),
                  pl.BlockSpec((CG_CHUNK, CG_HEADS), lambda i: (0, 0))],
        out_specs=pl.BlockSpec((tl, W), lambda i: (i, 0)),
        out_shape=jax.ShapeDtypeStruct((T, W), BF16),
        compiler_params=_params("parallel"), name="gmlp_gate",
    )(uv, uv, ln_g.reshape(1, W), ln_b.reshape(1, W), ws.astype(BF16), jnp.transpose(bs))


def _start_rows(n, per_iter, make_copy):
    assert n % per_iter == 0

    def issue(it, carry):
        for u in range(per_iter):
            make_copy(it, u).start(priority=u % 2)
        return carry

    lax.fori_loop(0, n // per_iter, issue, 0)


def _wait_rows(n, per_iter, make_copy):
    def drain(it, carry):
        for u in range(per_iter):
            make_copy(it, u).wait()
        return carry

    lax.fori_loop(0, n // per_iter, drain, 0)


def _gather_kernel(nu_ref, tok_ref, nxt_ref, f_hbm, o_ref, buf, sem):
    i = pl.program_id(0)
    tb, D = o_ref.shape
    slot = i % 2

    def row_copy(idx_ref, s):
        def make(it, u):
            return pltpu.make_async_copy(f_hbm.at[pl.ds(idx_ref[0, 0, it * F32_SUBLANES + u], 1), :],
                                         buf.at[s, it, pl.ds(u, 1), :], sem.at[s])
        return make

    @pl.when(jnp.logical_and(i == 0, nu_ref[0] > 0))
    def _():
        _start_rows(tb, F32_SUBLANES, row_copy(tok_ref, 0))

    @pl.when(i + 1 < nu_ref[0])
    def _():
        _start_rows(tb, F32_SUBLANES, row_copy(nxt_ref, 1 - slot))

    @pl.when(i < nu_ref[0])
    def _():
        _wait_rows(tb, F32_SUBLANES, row_copy(tok_ref, slot))
        o_ref[...] = buf[slot].reshape(tb, D).astype(o_ref.dtype)

    @pl.when(i >= nu_ref[0])
    def _():
        o_ref[...] = jnp.zeros(o_ref.shape, o_ref.dtype)


def _moe_gather(f, row_tok, n_used):
    T, D = f.shape
    P = row_tok.shape[0]
    nb = P // MOE_TM
    tok3 = row_tok.reshape(nb, 1, MOE_TM)
    grid_spec = pltpu.PrefetchScalarGridSpec(
        num_scalar_prefetch=1, grid=(nb,),
        in_specs=[pl.BlockSpec((1, 1, MOE_TM), lambda i, nu: (i, 0, 0), memory_space=pltpu.SMEM),
                  pl.BlockSpec((1, 1, MOE_TM), lambda i, nu: (jnp.minimum(i + 1, nb - 1), 0, 0),
                               memory_space=pltpu.SMEM),
                  pl.BlockSpec(memory_space=pl.ANY)],
        out_specs=pl.BlockSpec((MOE_TM, D), lambda i, nu: (i, 0)),
        scratch_shapes=[pltpu.VMEM((2, MOE_TM // F32_SUBLANES, F32_SUBLANES, D), F32),
                        pltpu.SemaphoreType.DMA((2,))])
    return pl.pallas_call(
        _gather_kernel, grid_spec=grid_spec,
        out_shape=jax.ShapeDtypeStruct((P, D), BF16),
        compiler_params=_params("arbitrary"), name="moe_gather",
    )(n_used, tok3, tok3, f)


def _expert_changed(be_ref):
    i = pl.program_id(1)
    return jnp.logical_or(i == 0, be_ref[i] != be_ref[jnp.maximum(i - 1, 0)])


def _gu_kernel(be_ref, nu_ref, x_ref, wg_ref, wu_ref, bg_ref, bu_ref, o_ref, wg_bf, wu_bf):
    @pl.when(pl.program_id(1) < nu_ref[0])
    def _():
        @pl.when(_expert_changed(be_ref))
        def _():
            wg_bf[...] = wg_ref[0].astype(BF16)
            wu_bf[...] = wu_ref[0].astype(BF16)

        xv = x_ref[...]
        gv = jnp.dot(xv, wg_bf[...], preferred_element_type=F32) + bg_ref[0]
        uv = jnp.dot(xv, wu_bf[...], preferred_element_type=F32) + bu_ref[0]
        gv = jnp.minimum(gv, SWIGLU_LIMIT)
        uv = jnp.clip(uv, -SWIGLU_LIMIT, SWIGLU_LIMIT)
        act = gv * (1.0 / (1.0 + jnp.exp(-SWIGLU_ALPHA * gv))) * (uv + 1.0)
        o_ref[...] = act.astype(o_ref.dtype)

    @pl.when(pl.program_id(1) >= nu_ref[0])
    def _():
        o_ref[...] = jnp.zeros(o_ref.shape, o_ref.dtype)


def _moe_gate_up(xs, block_e, n_used, w_gu, b_gu, *, tn=1024):
    P, D = xs.shape
    F = w_gu.shape[2] // 2
    nj = F // tn
    nb = P // MOE_TM
    b3 = b_gu.reshape(-1, 1, 2 * F)
    grid_spec = pltpu.PrefetchScalarGridSpec(
        num_scalar_prefetch=2, grid=(nj, nb),
        in_specs=[pl.BlockSpec((MOE_TM, D), lambda j, i, be, nu: (i, 0)),
                  pl.BlockSpec((1, D, tn), lambda j, i, be, nu: (be[i], 0, j)),
                  pl.BlockSpec((1, D, tn), lambda j, i, be, nu: (be[i], 0, nj + j)),
                  pl.BlockSpec((1, 1, tn), lambda j, i, be, nu: (be[i], 0, j)),
                  pl.BlockSpec((1, 1, tn), lambda j, i, be, nu: (be[i], 0, nj + j))],
        out_specs=pl.BlockSpec((MOE_TM, tn), lambda j, i, be, nu: (i, j)),
        scratch_shapes=[pltpu.VMEM((D, tn), BF16), pltpu.VMEM((D, tn), BF16)])
    return pl.pallas_call(
        _gu_kernel, grid_spec=grid_spec,
        out_shape=jax.ShapeDtypeStruct((P, F), BF16),
        compiler_params=_params("arbitrary", "arbitrary"), name="moe_gate_up",
    )(block_e, n_used, xs, w_gu, w_gu, b3, b3)


def _down_kernel(be_ref, nu_ref, a_ref, w_ref, b_ref, o_ref, w_bf):
    @pl.when(pl.program_id(1) < nu_ref[0])
    def _():
        @pl.when(_expert_changed(be_ref))
        def _():
            w_bf[...] = w_ref[0].astype(BF16)

        o_ref[...] = jnp.dot(a_ref[...], w_bf[...], preferred_element_type=F32) + b_ref[0]

    @pl.when(pl.program_id(1) >= nu_ref[0])
    def _():
        o_ref[...] = jnp.zeros(o_ref.shape, o_ref.dtype)


def _moe_down(act, block_e, n_used, w_down, b_down, *, tn=1024):
    P, F = act.shape
    D = w_down.shape[2]
    nb = P // MOE_TM
    grid_spec = pltpu.PrefetchScalarGridSpec(
        num_scalar_prefetch=2, grid=(D // tn, nb),
        in_specs=[pl.BlockSpec((MOE_TM, F), lambda j, i, be, nu: (i, 0)),
                  pl.BlockSpec((1, F, tn), lambda j, i, be, nu: (be[i], 0, j)),
                  pl.BlockSpec((1, 1, tn), lambda j, i, be, nu: (be[i], 0, j))],
        out_specs=pl.BlockSpec((MOE_TM, tn), lambda j, i, be, nu: (i, j)),
        scratch_shapes=[pltpu.VMEM((F, tn), BF16)])
    return pl.pallas_call(
        _down_kernel, grid_spec=grid_spec,
        out_shape=jax.ShapeDtypeStruct((P, D), F32),
        compiler_params=_params("arbitrary", "arbitrary"), name="moe_down",
    )(block_e, n_used, act, w_down, b_down.reshape(-1, 1, D))


def _combine_kernel(dest_ref, nxt_ref, gt_ref, x_ref, g_ref, sc_ref, sh_ref, ys_hbm, *rest, want_x):
    if want_x:
        xo_ref, h_ref, buf, sem = rest
    else:
        h_ref, buf, sem = rest
    i = pl.program_id(0)
    tb, D = x_ref.shape
    per_iter = F32_SUBLANES * TOP_K
    slot = i % 2

    def row_copy(idx_ref, s):
        def make(it, u):
            return pltpu.make_async_copy(ys_hbm.at[pl.ds(idx_ref[0, 0, it * per_iter + u], 1), :],
                                         buf.at[s, u % TOP_K, it, pl.ds(u // TOP_K, 1), :], sem.at[s])
        return make

    @pl.when(i == 0)
    def _():
        _start_rows(tb * TOP_K, per_iter, row_copy(dest_ref, 0))

    @pl.when(i + 1 < pl.num_programs(0))
    def _():
        _start_rows(tb * TOP_K, per_iter, row_copy(nxt_ref, 1 - slot))

    _wait_rows(tb * TOP_K, per_iter, row_copy(dest_ref, slot))
    moe = None
    for k in range(TOP_K):
        term = buf[slot, k].reshape(tb, D) * gt_ref[:, k:k + 1]
        moe = term if moe is None else moe + term
    xv = x_ref[...] + g_ref[0] * moe
    if want_x:
        xo_ref[...] = xv
    h_ref[...] = _rms_mod(xv, sc_ref[0], sh_ref[0]).astype(h_ref.dtype)


def _moe_combine(ys, dest, gates, x2, g, sc, sh, L, *, want_x, out_dtype, tb=256):
    T, D = x2.shape
    tb = min(tb, L)
    nl = L // tb
    mod = pl.BlockSpec((1, 1, D), lambda i: (i // nl, 0, 0))
    row = pl.BlockSpec((tb, D), lambda i: (i, 0))
    out_specs = [row, row] if want_x else [row]
    out_shape = ([jax.ShapeDtypeStruct((T, D), F32)] if want_x else []) + [jax.ShapeDtypeStruct((T, D), out_dtype)]
    nb = T // tb
    dest3 = dest.reshape(nb, 1, tb * TOP_K)
    return pl.pallas_call(
        functools.partial(_combine_kernel, want_x=want_x), grid=(nb,),
        in_specs=[pl.BlockSpec((1, 1, tb * TOP_K), lambda i: (i, 0, 0), memory_space=pltpu.SMEM),
                  pl.BlockSpec((1, 1, tb * TOP_K), lambda i: (jnp.minimum(i + 1, nb - 1), 0, 0),
                               memory_space=pltpu.SMEM),
                  pl.BlockSpec((tb, TOP_K), lambda i: (i, 0)),
                  row, mod, mod, mod, pl.BlockSpec(memory_space=pl.ANY)],
        out_specs=out_specs, out_shape=out_shape,
        scratch_shapes=[pltpu.VMEM((2, TOP_K, tb // F32_SUBLANES, F32_SUBLANES, D), F32),
                        pltpu.SemaphoreType.DMA((2,))],
        compiler_params=_params("arbitrary"), name="moe_combine",
    )(dest3, dest3, gates, x2, g, sc, sh, ys)


def _moe_routing(top_i):
    T = top_i.shape[0]
    A = T * TOP_K
    P = A + N_EXPERTS * MOE_TM
    nb = P // MOE_TM
    flat_e = top_i.reshape(A)
    onehot = (flat_e[:, None] == jnp.arange(N_EXPERTS, dtype=jnp.int32)[None, :]).astype(jnp.int32)
    cs = jnp.cumsum(onehot, axis=0)
    rank = jnp.take_along_axis(cs, flat_e[:, None], axis=1)[:, 0] - 1
    counts = cs[-1]
    padded = (counts + MOE_TM - 1) // MOE_TM * MOE_TM
    pad_end = jnp.cumsum(padded)
    pad_start = pad_end - padded
    dest = (pad_start[flat_e] + rank).astype(jnp.int32)
    block_e = jnp.minimum(jnp.searchsorted(pad_end, jnp.arange(nb, dtype=jnp.int32) * MOE_TM, side="right"),
                          N_EXPERTS - 1).astype(jnp.int32)
    n_used = (pad_end[-1:] // MOE_TM).astype(jnp.int32)
    _, sorted_tok = lax.sort((dest, jnp.arange(A, dtype=jnp.int32) // TOP_K), num_keys=1)
    start = jnp.cumsum(counts) - counts
    row = jnp.arange(P, dtype=jnp.int32)
    row_e = jnp.repeat(block_e, MOE_TM)
    off = row - pad_start[row_e]
    src = jnp.clip(start[row_e] + off, 0, A - 1)
    row_tok = jnp.where(off < counts[row_e], sorted_tok[src], row % T).astype(jnp.int32)
    return dest, row_tok, block_e, n_used


def _moe(x2, m, g1, sc2, sh2, g2, nsc, nsh, router_w, router_b, w_gu, b_gu, w_down, b_down, L,
         *, layer, want_x, out_dtype):
    xo, f, top_i, gates = _res_norm_router(x2, m, g1, sc2, sh2, jnp.transpose(router_w),
                                           router_b.reshape(N_EXPERTS, 1), L)
    dest, row_tok, block_e, n_used = _moe_routing(top_i)
    xs = _moe_gather(f, row_tok, n_used)
    block_w = block_e + layer * N_EXPERTS
    act = _moe_gate_up(xs, block_w, n_used, w_gu.reshape((-1,) + w_gu.shape[2:]), b_gu)
    ys = _moe_down(act, block_w, n_used, w_down.reshape((-1,) + w_down.shape[2:]), b_down)
    return _moe_combine(ys, dest, gates, xo, g2, nsc, nsh, L, want_x=want_x, out_dtype=out_dtype)


def _ssd_tri():
    l = np.arange(SSD_CHUNK)[:, None]
    j = np.arange(SSD_CHUNK)[None, :]
    return jnp.asarray(np.stack([(j <= l), (j >= l)]).astype(np.float32))


def _ssd_stream(h_bf, w_main, w_dt, conv_w, conv_b, a_log, dt_bias, h0, Bsz, L, width):
    T = h_bf.shape[0]
    H = width // SSD_HEAD_DIM
    p_main = _matmul([(h_bf, w_main)], tm=1024, tn=1024, out_dtype=BF16, name="even_in_proj")
    dt_raw = _matmul([(h_bf, w_dt)], tm=1024, tn=w_dt.shape[1], out_dtype=F32, fold=3, name="dt_proj")
    dt2 = jnp.transpose(dt_raw[:, :2 * H].reshape(T, 2, H), (1, 0, 2))
    xbc = _ssd_conv(p_main, conv_w, conv_b, Bsz, L)
    y, h_last = _ssd_scan(xbc, dt2, _ssd_tri(), a_log, dt_bias, h0, Bsz, L, width)
    return p_main, xbc, y, h_last


def kernel(x, c, ctx, c_ctx, ada_w, ada_b, even_in_w, ssd_conv_w, ssd_conv_b, ssd_a_log, ssd_dt_bias,
           ssd_d, ssd_norm_w, pool_w, pool_b, pool_scale, even_out_w, cg_in_w, cg_ln_g, cg_ln_b, cg_ws,
           cg_bs, cg_out_w, router_w, router_b, moe_w_gu, moe_b_gu, moe_w_down, moe_b_down, final_norm_w):
    Bsz, S, D = x.shape
    Lc = ctx.shape[1]
    T = Bsz * S
    width = D
    H = width // SSD_HEAD_DIM
    gn = SSD_GROUPS * SSD_STATE
    x2 = x.reshape(T, D)
    ctx2 = ctx.reshape(Bsz * Lc, D)

    cin = jnp.zeros((8, D), F32).at[:Bsz].set(c).at[Bsz].set(c_ctx)
    mods = [_matmul([(cin, ada_w[i])], tm=8, tn=1024, out_dtype=F32, bias=ada_b[i], pre_act="silu",
                    precision=HIGHEST, name="ada_mod") for i in range(DEPTH)]

    def lat_mods(i):
        return [mods[i][:Bsz, k * D:(k + 1) * D].reshape(Bsz, 1, D) for k in range(6)]

    sh1, sc1, g1, sh2, sc2, g2 = lat_mods(0)
    sh_c = jnp.broadcast_to(mods[0][Bsz, :D].reshape(1, 1, D), (Bsz, 1, D))
    sc_c = jnp.broadcast_to(mods[0][Bsz, D:2 * D].reshape(1, 1, D), (Bsz, 1, D))
    w_in = even_in_w[0]
    o_dt = 2 * width + 2 * gn
    w_main = jnp.concatenate([w_in[:, :o_dt], w_in[:, o_dt + 2 * H:]], axis=1).astype(BF16)
    w_dt = _split3(jnp.pad(w_in[:, o_dt:o_dt + 2 * H], ((0, 0), (0, 128 - 2 * H))))
    ssd_p = (w_main, w_dt, ssd_conv_w[0], ssd_conv_b[0], ssd_a_log[0], ssd_dt_bias[0])

    h_c = _norm_mod(ctx2, sc_c, sh_c, Lc)
    h0 = jnp.zeros((2, Bsz, SSD_GROUPS, SSD_STATE, width // SSD_GROUPS), F32)
    _, _, _, ctx_state = _ssd_stream(h_c, *ssd_p, h0, Bsz, Lc, width)

    h_l = _norm_mod(x2, sc1, sh1, S)
    p_main, xbc, y, _ = _ssd_stream(h_l, *ssd_p, ctx_state, Bsz, S, width)
    y_ssd = _ssd_out(y, xbc, p_main, ssd_d[0], ssd_norm_w[0])
    y_pool = _pool_mixer(p_main, pool_w[0], pool_b[0], pool_scale[0], Bsz, S)
    w_out = even_out_w[0].astype(BF16)
    m = _matmul([(y_ssd, w_out[:width]), (y_pool, w_out[width:])], tm=1024, tn=1024, out_dtype=F32,
                name="even_out_proj")

    nsh1, nsc1, ng1, nsh2, nsc2, ng2 = lat_mods(1)
    x2, h_l = _moe(x2, m, g1, sc2, sh2, g2, nsc1, nsh1, router_w[0], router_b[0], moe_w_gu, moe_b_gu,
                   moe_w_down, moe_b_down, S, layer=0, want_x=True, out_dtype=BF16)

    uv = _matmul([(h_l, cg_in_w[0].astype(BF16))], tm=1024, tn=1024, out_dtype=BF16, act="gelu",
                 name="cg_in_proj")
    gated = _gmlp_gate(uv, cg_ln_g[0], cg_ln_b[0], cg_ws[0], cg_bs[0])
    m = _matmul([(gated, cg_out_w[0].astype(BF16))], tm=1024, tn=1024, out_dtype=F32, name="cg_out_proj")
    fin_sc = jnp.broadcast_to((final_norm_w - 1.0).reshape(1, 1, D), (Bsz, 1, D))
    fin_sh = jnp.zeros((Bsz, 1, D), F32)
    (out,) = _moe(x2, m, ng1, nsc2, nsh2, ng2, fin_sc, fin_sh, router_w[1], router_b[1], moe_w_gu,
                  moe_b_gu, moe_w_down, moe_b_down, S, layer=1, want_x=False, out_dtype=F32)
    return out.reshape(Bsz, S, D)
```
